```python
import math
import jax, jax.numpy as jnp
from jax import lax
import numpy as np


D_MODEL = 1024
BATCH = 4
SEQ = 8192
DEPTH = 2

MEM_LEN = 256
SSM_WIDTH = D_MODEL // 2
SSM_GROUP = 16
SSM_GROUPS = SSM_WIDTH // SSM_GROUP
SSM_STATE = 64
DIFF_WIDTH = D_MODEL - SSM_WIDTH
DIFF_HEADS = 4
DIFF_V_DIM = DIFF_WIDTH // DIFF_HEADS
DIFF_QK_DIM = DIFF_V_DIM // 2
MIX_WIDTH = SSM_WIDTH + DIFF_WIDTH
IN_PROJ_WIDTH = SSM_WIDTH + 3 * DIFF_WIDTH
XATTN_HEADS = 4
XATTN_HEAD_DIM = D_MODEL // XATTN_HEADS
D_FF = ((8 * D_MODEL // 3 + 127) // 128) * 128
NUM_BUCKETS = 32
MAX_DISTANCE = 128
Q_BLOCK = 128
EPS = 1e-6

kernel_name = "hybrid_s5_diffattn_macaron_decoder"


def rms_norm(x, gain):
    x32 = x.astype(jnp.float32)
    y = x32 * lax.rsqrt(jnp.mean(x32 * x32, axis=-1, keepdims=True) + EPS)
    return (y * gain.astype(jnp.float32)).astype(x.dtype)


def swiglu(x, w_gate, w_up, w_down):
    return (jax.nn.silu(x @ w_gate) * (x @ w_up)) @ w_down


def rel_bucket(rel):
    n = jnp.maximum(rel, 0)
    max_exact = NUM_BUCKETS // 2
    n_f = jnp.maximum(n, 1).astype(jnp.float32)
    large = max_exact + (jnp.log(n_f / max_exact) / math.log(MAX_DISTANCE / max_exact)
                         * (NUM_BUCKETS - max_exact)).astype(jnp.int32)
    large = jnp.minimum(large, NUM_BUCKETS - 1)
    return jnp.where(n < max_exact, n, large)


def s5_scan(u, lam_re, lam_im, b_re, b_im, c_re, c_im, d, log_dt):
    f32 = jnp.float32
    u32 = u.astype(f32)
    lam = lax.complex(jnp.minimum(lam_re.astype(f32), -1e-4), lam_im.astype(f32))
    dt = jnp.exp(log_dt.astype(f32))[:, None]
    lam_bar = jnp.exp(lam * dt)
    b = lax.complex(b_re.astype(f32), b_im.astype(f32))
    b_bar = ((lam_bar - 1.0) / lam)[:, :, None] * b
    bu = jnp.einsum('gph,blgh->blgp', b_bar, u32.astype(jnp.complex64))
    a = jnp.broadcast_to(lam_bar, (1, u.shape[1]) + lam_bar.shape)

    def combine(e_i, e_j):
        a_i, b_i = e_i
        a_j, b_j = e_j
        return a_j * a_i, a_j * b_i + b_j

    _, states = lax.associative_scan(combine, (a, bu), axis=1)
    c = lax.complex(c_re.astype(f32), c_im.astype(f32))
    return jnp.real(jnp.einsum('ghp,blgp->blgh', c, states)) + d.astype(f32) * u32


def diff_attention(q, k, v, lam, rel_bias):
    f32 = jnp.float32
    B, L = q.shape[0], q.shape[1]
    nb = L // Q_BLOCK
    scale = DIFF_QK_DIM ** -0.5
    k32 = k.astype(f32)
    v32 = v.astype(f32)
    q_blocks = q.astype(f32).reshape(B, nb, Q_BLOCK, DIFF_HEADS, 2, DIFF_QK_DIM).transpose(1, 0, 2, 3, 4, 5)
    starts = jnp.arange(nb, dtype=jnp.int32) * Q_BLOCK
    k_pos = jnp.arange(L, dtype=jnp.int32)
    table = rel_bias.astype(f32)

    def block(args):
        qb, start = args
        rel = (start + jnp.arange(Q_BLOCK, dtype=jnp.int32))[:, None] - k_pos[None, :]
        bias = table[rel_bucket(rel)].transpose(2, 0, 1)
        s = jnp.einsum('bqhmd,bkhmd->bhmqk', qb, k32) * scale + bias[None, :, None]
        s = jnp.where(rel >= 0, s, -jnp.inf)
        p = jax.nn.softmax(s, axis=-1)
        a = p[:, :, 0] - lam * p[:, :, 1]
        return jnp.einsum('bhqk,bkhe->bqhe', a, v32)

    out = lax.map(block, (q_blocks, starts))
    return out.transpose(1, 0, 2, 3, 4).reshape(B, L, DIFF_HEADS, DIFF_V_DIM)


def hybrid_mixer(u, w_in, lam_re, lam_im, b_re, b_im, c_re, c_im, d, log_dt, w_glu, b_glu,
                 ssm_out_norm, lq1, lk1, lq2, lk2, subln, w_out, rel_bias, lam_init):
    f32 = jnp.float32
    B, L, _ = u.shape
    z = u @ w_in
    o0 = SSM_WIDTH
    ssm_in = z[..., :o0].reshape(B, L, SSM_GROUPS, SSM_GROUP)
    q = z[..., o0:o0 + DIFF_WIDTH].reshape(B, L, DIFF_HEADS, 2, DIFF_QK_DIM)
    k = z[..., o0 + DIFF_WIDTH:o0 + 2 * DIFF_WIDTH].reshape(B, L, DIFF_HEADS, 2, DIFF_QK_DIM)
    v = z[..., o0 + 2 * DIFF_WIDTH:].reshape(B, L, DIFF_HEADS, DIFF_V_DIM)

    y = s5_scan(ssm_in, lam_re, lam_im, b_re, b_im, c_re, c_im, d, log_dt).reshape(B, L, SSM_WIDTH)
    y = jax.nn.gelu(y).astype(u.dtype) @ w_glu + b_glu
    y_val, y_gate = jnp.split(y, 2, axis=-1)
    y_ssm = rms_norm(y_val * jax.nn.sigmoid(y_gate), ssm_out_norm)

    lam = (jnp.exp(jnp.sum(lq1.astype(f32) * lk1.astype(f32)))
           - jnp.exp(jnp.sum(lq2.astype(f32) * lk2.astype(f32))) + lam_init)
    o = diff_attention(q, k, v, lam, rel_bias)
    o = rms_norm(o, subln) * (1.0 - lam_init)
    y_attn = o.reshape(B, L, DIFF_WIDTH).astype(u.dtype)

    return jnp.concatenate([y_ssm.astype(u.dtype), y_attn], axis=-1) @ w_out


def memory_cross_attention(hq, m, wq, wkv, wo):
    f32 = jnp.float32
    B, L, _ = hq.shape
    q = (hq @ wq).reshape(B, L, XATTN_HEADS, XATTN_HEAD_DIM)
    kv = (m @ wkv).reshape(B, m.shape[1], 2, XATTN_HEADS, XATTN_HEAD_DIM)
    s = jnp.einsum('bqhd,bkhd->bhqk', q.astype(f32), kv[:, :, 0].astype(f32)) * XATTN_HEAD_DIM ** -0.5
    p = jax.nn.softmax(s, axis=-1)
    o = jnp.einsum('bhqk,bkhd->bqhd', p, kv[:, :, 1].astype(f32)).astype(hq.dtype)
    return o.reshape(B, L, D_MODEL) @ wo


def setup_inputs(seed: int = 0) -> dict:
    key = jax.random.key(seed)
    ks = jax.random.split(key, 40)
    f32 = jnp.float32

    def nrm(k, shape, scale):
        return jax.random.normal(k, shape, f32) * scale

    def gain(k, shape):
        return 1.0 + 0.01 * jax.random.normal(k, shape, f32)

    Lr, D, F, G, P, H = DEPTH, D_MODEL, D_FF, SSM_GROUPS, SSM_STATE, SSM_GROUP
    lam_im0 = jnp.broadcast_to(math.pi * jnp.arange(P, dtype=f32), (Lr, G, P))
    return {
        "x": nrm(ks[0], (BATCH, SEQ, D), 1.0),
        "mem": nrm(ks[1], (BATCH, MEM_LEN, D), 1.0),
        "rel_bias": nrm(ks[2], (NUM_BUCKETS, DIFF_HEADS), 0.5),
        "ffn1_norm": gain(ks[3], (Lr, D)),
        "ffn1_w_gate": nrm(ks[4], (Lr, D, F), D ** -0.5),
        "ffn1_w_up": nrm(ks[5], (Lr, D, F), D ** -0.5),
        "ffn1_w_down": nrm(ks[6], (Lr, F, D), F ** -0.5),
        "mix_norm": gain(ks[7], (Lr, D)),
        "w_in": nrm(ks[8], (Lr, D, IN_PROJ_WIDTH), D ** -0.5),
        "ssm_lambda_re": -0.5 + 0.01 * jax.random.normal(ks[9], (Lr, G, P), f32),
        "ssm_lambda_im": lam_im0 + 0.01 * jax.random.normal(ks[10], (Lr, G, P), f32),
        "ssm_b_re": nrm(ks[11], (Lr, G, P, H), (2.0 * H) ** -0.5),
        "ssm_b_im": nrm(ks[12], (Lr, G, P, H), (2.0 * H) ** -0.5),
        "ssm_c_re": nrm(ks[13], (Lr, G, H, P), (2.0 * P) ** -0.5),
        "ssm_c_im": nrm(ks[14], (Lr, G, H, P), (2.0 * P) ** -0.5),
        "ssm_d": nrm(ks[15], (Lr, G, H), 1.0),
        "ssm_log_dt": jax.random.uniform(ks[16], (Lr, G), f32, math.log(1e-3), math.log(1e-1)),
        "ssm_w_glu": nrm(ks[17], (Lr, SSM_WIDTH, 2 * SSM_WIDTH), SSM_WIDTH ** -0.5),
        "ssm_b_glu": nrm(ks[18], (Lr, 2 * SSM_WIDTH), 0.01),
        "ssm_out_norm": gain(ks[19], (Lr, SSM_WIDTH)),
        "diff_lambda_q1": nrm(ks[20], (Lr, DIFF_QK_DIM), 0.1),
        "diff_lambda_k1": nrm(ks[21], (Lr, DIFF_QK_DIM), 0.1),
        "diff_lambda_q2": nrm(ks[22], (Lr, DIFF_QK_DIM), 0.1),
        "diff_lambda_k2": nrm(ks[23], (Lr, DIFF_QK_DIM), 0.1),
        "diff_subln": gain(ks[24], (Lr, DIFF_V_DIM)),
        "w_out": nrm(ks[25], (Lr, MIX_WIDTH, D), MIX_WIDTH ** -0.5),
        "xattn_norm": gain(ks[26], (Lr, D)),
        "mem_norm": gain(ks[27], (Lr, D)),
        "xattn_wq": nrm(ks[28], (Lr, D, D), D ** -0.5),
        "xattn_wkv": nrm(ks[29], (Lr, D, 2 * D), D ** -0.5),
        "xattn_wo": nrm(ks[30], (Lr, D, D), D ** -0.5),
        "ffn2_norm": gain(ks[31], (Lr, D)),
        "ffn2_w_gate": nrm(ks[32], (Lr, D, F), D ** -0.5),
        "ffn2_w_up": nrm(ks[33], (Lr, D, F), D ** -0.5),
        "ffn2_w_down": nrm(ks[34], (Lr, F, D), F ** -0.5),
        "final_norm": gain(ks[35], (D,)),
    }


def reference(x, mem, rel_bias, ffn1_norm, ffn1_w_gate, ffn1_w_up, ffn1_w_down, mix_norm, w_in,
              ssm_lambda_re, ssm_lambda_im, ssm_b_re, ssm_b_im, ssm_c_re, ssm_c_im, ssm_d,
              ssm_log_dt, ssm_w_glu, ssm_b_glu, ssm_out_norm, diff_lambda_q1, diff_lambda_k1,
              diff_lambda_q2, diff_lambda_k2, diff_subln, w_out, xattn_norm, mem_norm, xattn_wq,
              xattn_wkv, xattn_wo, ffn2_norm, ffn2_w_gate, ffn2_w_up, ffn2_w_down, final_norm):
    h = x
    for l in range(DEPTH):
        lam_init = 0.8 - 0.6 * math.exp(-0.3 * l)
        h = h + 0.5 * swiglu(rms_norm(h, ffn1_norm[l]), ffn1_w_gate[l], ffn1_w_up[l], ffn1_w_down[l])
        h = h + hybrid_mixer(rms_norm(h, mix_norm[l]), w_in[l], ssm_lambda_re[l], ssm_lambda_im[l],
                             ssm_b_re[l], ssm_b_im[l], ssm_c_re[l], ssm_c_im[l], ssm_d[l],
                             ssm_log_dt[l], ssm_w_glu[l], ssm_b_glu[l], ssm_out_norm[l],
                             diff_lambda_q1[l], diff_lambda_k1[l], diff_lambda_q2[l],
                             diff_lambda_k2[l], diff_subln[l], w_out[l], rel_bias, lam_init)
        h = h + memory_cross_attention(rms_norm(h, xattn_norm[l]), rms_norm(mem, mem_norm[l]),
                                       xattn_wq[l], xattn_wkv[l], xattn_wo[l])
        h = h + 0.5 * swiglu(rms_norm(h, ffn2_norm[l]), ffn2_w_gate[l], ffn2_w_up[l], ffn2_w_down[l])
    return rms_norm(h, final_norm)
```

```python
import functools
import math

import numpy as np
import jax
import jax.numpy as jnp
from jax import lax
from jax.experimental import pallas as pl
from jax.experimental.pallas import tpu as pltpu

F32 = jnp.float32
BF16 = jnp.bfloat16

EPS = 1e-6
SSM_GROUP = 16
DIFF_HEADS = 4
XATTN_HEADS = 4
MAX_DISTANCE = 128
SUBLANES = 8
MXU_DIM = 256
VMEM_LIMIT_BYTES = 56 * 1024 * 1024

TOKEN_TILE = 512
FFN_CHUNK = 512
ATTN_TILE = 256
S5_TILE = 512
S5_COLS = 1024


def _rms(x, gain):
    ms = jnp.mean(x * x, axis=-1, keepdims=True)
    return x * lax.rsqrt(ms + EPS) * gain


def _dot(a, b):
    return jnp.dot(a, b, preferred_element_type=F32)


def _dot_nt(a, b):
    return lax.dot_general(a, b, (((1,), (1,)), ((), ())), preferred_element_type=F32)


def _resident(shape):
    zeros = (0,) * len(shape)
    return pl.BlockSpec(shape, lambda *_: zeros, pipeline_mode=pl.Buffered(1))


def _params(*semantics):
    return pltpu.CompilerParams(dimension_semantics=semantics, vmem_limit_bytes=VMEM_LIMIT_BYTES)


def _ffn_kernel(h_ref, g_ref, wg_ref, wu_ref, wd_ref, *rest, final):
    if final:
        fg_ref, o_ref, a_ref = rest
    else:
        o_ref, a_ref = rest
    n = _rms(h_ref[...], g_ref[...]).astype(BF16)
    d_ff = wg_ref.shape[1]
    for c0 in range(0, d_ff, FFN_CHUNK):
        c1 = min(c0 + FFN_CHUNK, d_ff)
        g = _dot(n, wg_ref[:, c0:c1])
        u = _dot(n, wu_ref[:, c0:c1])
        a_ref[:, c0:c1] = (jax.nn.silu(g) * u).astype(BF16)
    out = h_ref[...] + 0.5 * _dot(a_ref[...], wd_ref[...])
    if final:
        out = _rms(out, fg_ref[...])
    o_ref[...] = out


def _ffn(h, gain, w_gate, w_up, w_down, final_gain=None):
    t, d = h.shape
    d_ff = w_gate.shape[1]
    tm = TOKEN_TILE
    final = final_gain is not None
    row = pl.BlockSpec((tm, d), lambda i: (i, 0))
    in_specs = [row, _resident((1, d)), _resident((d, d_ff)), _resident((d, d_ff)), _resident((d_ff, d))]
    args = [h, gain.reshape(1, d), w_gate.astype(BF16), w_up.astype(BF16), w_down.astype(BF16)]
    if final:
        in_specs.append(_resident((1, d)))
        args.append(final_gain.reshape(1, d))
    return pl.pallas_call(
        functools.partial(_ffn_kernel, final=final),
        grid=(t // tm,),
        in_specs=in_specs,
        out_specs=row,
        out_shape=jax.ShapeDtypeStruct((t, d), F32),
        scratch_shapes=[pltpu.VMEM((tm, d_ff), BF16)],
        compiler_params=_params("parallel"),
        name="ffn_final" if final else "ffn",
    )(*args)


def _inproj_kernel(h_ref, g_ref, w_ref, u_ref, q_ref, k_ref, v_ref, *, q_scale):
    n = _rms(h_ref[...], g_ref[...]).astype(BF16)
    w = u_ref.shape[1]
    u_ref[...] = _dot(n, w_ref[:, 0:w])
    q_ref[...] = (_dot(n, w_ref[:, w:2 * w]) * q_scale).astype(BF16)
    k_ref[...] = _dot(n, w_ref[:, 2 * w:3 * w]).astype(BF16)
    v_ref[...] = _dot(n, w_ref[:, 3 * w:4 * w]).astype(BF16)


def _inproj(h, gain, w_in, width, q_scale):
    t, d = h.shape
    tm = TOKEN_TILE
    assert w_in.shape[1] == 4 * width
    out_spec = pl.BlockSpec((tm, width), lambda i: (i, 0))
    return pl.pallas_call(
        functools.partial(_inproj_kernel, q_scale=q_scale),
        grid=(t // tm,),
        in_specs=[pl.BlockSpec((tm, d), lambda i: (i, 0)), _resident((1, d)), _resident((d, 4 * width))],
        out_specs=[out_spec] * 4,
        out_shape=[jax.ShapeDtypeStruct((t, width), F32)] + [jax.ShapeDtypeStruct((t, width), BF16)] * 3,
        compiler_params=_params("parallel"),
        name="inproj",
    )(h, gain.reshape(1, d), w_in.astype(BF16))


def _s5_kernel(u_ref, perm3_ref, perm_t_ref, bre_ref, bim_ref, lre_ref, lim_ref, ltre_ref, ltim_ref,
               c_ref, d_ref, wglu_ref, bglu_ref, gn_ref, o_ref, x_ref, carry_re_ref, carry_im_ref):
    tl = u_ref.shape[1]
    tc = tl // SUBLANES
    width = u_ref.shape[2]
    n_kb = width // MXU_DIM
    sb = x_ref.shape[1] // (2 * n_kb)

    @pl.when(pl.program_id(1) == 0)
    def _():
        carry_re_ref[...] = jnp.zeros_like(carry_re_ref)
        carry_im_ref[...] = jnp.zeros_like(carry_im_ref)

    u = u_ref[0]
    u_hi = u.astype(BF16)
    rem = u - u_hi.astype(F32)
    u_mid = rem.astype(BF16)
    u_lo = (rem - u_mid.astype(F32)).astype(BF16)
    up = _dot(perm3_ref[...], jnp.concatenate([u_hi, u_mid, u_lo], axis=0))
    up_b = _dot(perm3_ref[:, :tl], u_hi).astype(BF16)

    for kb in range(n_kb):
        ub = up_b[:, kb * MXU_DIM:(kb + 1) * MXU_DIM]
        x_ref[:, 2 * kb * sb:(2 * kb + 1) * sb] = _dot(ub, bre_ref[kb])
        x_ref[:, (2 * kb + 1) * sb:(2 * kb + 2) * sb] = _dot(ub, bim_ref[kb])

    for kb in range(n_kb):
        for c0 in range(0, sb, S5_COLS):
            lam_cols = slice(kb * sb + c0, kb * sb + c0 + S5_COLS)
            re_cols = slice(2 * kb * sb + c0, 2 * kb * sb + c0 + S5_COLS)
            im_cols = slice((2 * kb + 1) * sb + c0, (2 * kb + 1) * sb + c0 + S5_COLS)
            lr = jnp.broadcast_to(lre_ref[:, lam_cols], (SUBLANES, S5_COLS))
            li = jnp.broadcast_to(lim_ref[:, lam_cols], (SUBLANES, S5_COLS))

            def step(j, xr, xi):
                r = pl.multiple_of(j * SUBLANES, SUBLANES)
                br = x_ref[pl.ds(r, SUBLANES), re_cols]
                bi = x_ref[pl.ds(r, SUBLANES), im_cols]
                return r, lr * xr - li * xi + br, lr * xi + li * xr + bi

            def end_state_body(j, x):
                _, nr, ni = step(j, *x)
                return nr, ni

            zero = jnp.zeros((SUBLANES, S5_COLS), F32)
            end_re, end_im = lax.fori_loop(0, tc, end_state_body, (zero, zero), unroll=4)

            ltr = ltre_ref[:, lam_cols]
            lti = ltim_ref[:, lam_cols]
            cr = carry_re_ref[:, lam_cols]
            ci = carry_im_ref[:, lam_cols]
            init_re, init_im = [], []
            for k in range(SUBLANES):
                init_re.append(cr)
                init_im.append(ci)
                cr, ci = (ltr * cr - lti * ci + end_re[k:k + 1], ltr * ci + lti * cr + end_im[k:k + 1])
            carry_re_ref[:, lam_cols] = cr
            carry_im_ref[:, lam_cols] = ci

            def store_body(j, x):
                r, nr, ni = step(j, *x)
                x_ref[pl.ds(r, SUBLANES), re_cols] = nr
                x_ref[pl.ds(r, SUBLANES), im_cols] = ni
                return nr, ni

            lax.fori_loop(0, tc, store_body,
                          (jnp.concatenate(init_re, axis=0), jnp.concatenate(init_im, axis=0)), unroll=4)

    parts = [_dot(x_ref[:, 2 * kb * sb:(2 * kb + 2) * sb].astype(BF16), c_ref[kb]) for kb in range(n_kb)]
    y = jnp.concatenate(parts, axis=1) + d_ref[...] * up
    z = _dot(jax.nn.gelu(y).astype(BF16), wglu_ref[...]) + bglu_ref[...]
    gated = z[:, :width] * jax.nn.sigmoid(z[:, width:])
    out = _rms(gated, gn_ref[...]).astype(BF16)
    o_ref[0] = _dot(perm_t_ref[...], out).astype(BF16)


def _s5_discretise(lam_re, lam_im, b_re, b_im, log_dt, n_sub):
    a = jnp.minimum(lam_re, -1e-4)
    dt = jnp.exp(log_dt)[:, None]
    za, zb = a * dt, lam_im * dt
    ea = jnp.exp(za)
    bar_re, bar_im = ea * jnp.cos(zb), ea * jnp.sin(zb)
    num_re = jnp.expm1(za) * jnp.cos(zb) - 2.0 * jnp.sin(0.5 * zb) ** 2
    num_im = bar_im
    den = a * a + lam_im * lam_im
    f_re = (num_re * a + num_im * lam_im) / den
    f_im = (num_im * a - num_re * lam_im) / den
    bb_re = f_re[:, :, None] * b_re - f_im[:, :, None] * b_im
    bb_im = f_re[:, :, None] * b_im + f_im[:, :, None] * b_re
    en = jnp.exp(za * n_sub)
    pow_re, pow_im = en * jnp.cos(zb * n_sub), en * jnp.sin(zb * n_sub)
    return bar_re, bar_im, bb_re, bb_im, pow_re, pow_im


def _s5(u, lam_re, lam_im, b_re, b_im, c_re, c_im, d, log_dt, w_glu, b_glu, out_norm):
    bsz, seq, width = u.shape
    n_groups, n_p = lam_re.shape
    gh = SSM_GROUP
    assert n_groups * gh == width and width % MXU_DIM == 0
    tl = min(S5_TILE, seq)
    tc = tl // SUBLANES
    n_state = n_groups * n_p
    n_kb = width // MXU_DIM
    gpb = n_groups // n_kb
    assert (gpb * n_p) % S5_COLS == 0
    bar_re, bar_im, bb_re, bb_im, pow_re, pow_im = _s5_discretise(lam_re, lam_im, b_re, b_im, log_dt, tc)

    eye = jnp.eye(gpb, dtype=F32)

    def in_blocks(bb):
        bb = bb.reshape(n_kb, gpb, n_p, gh)
        return jnp.einsum('kgph,gj->kghjp', bb, eye).reshape(n_kb, gpb * gh, gpb * n_p).astype(BF16)

    def out_blocks(c):
        c = c.reshape(n_kb, gpb, gh, n_p)
        return jnp.einsum('kghp,gj->kgpjh', c, eye).reshape(n_kb, gpb * n_p, gpb * gh).astype(BF16)

    rows = np.arange(tl)
    perm_np = np.zeros((tl, tl), np.float32)
    perm_np[rows, (rows % SUBLANES) * tc + rows // SUBLANES] = 1.0
    perm3 = jnp.asarray(np.concatenate([perm_np] * 3, axis=1), BF16)
    perm_t = jnp.asarray(perm_np.T, BF16)

    flat = lambda x: x.reshape(1, n_state)
    c_blocks = jnp.concatenate([out_blocks(c_re), out_blocks(-c_im)], axis=1)
    args = [u, perm3, perm_t, in_blocks(bb_re), in_blocks(bb_im), flat(bar_re), flat(bar_im),
            flat(pow_re), flat(pow_im), c_blocks, d.reshape(1, width),
            w_glu.astype(BF16), b_glu.reshape(1, 2 * width), out_norm.reshape(1, width)]
    tile = pl.BlockSpec((1, tl, width), lambda b, l: (b, l, 0))
    in_specs = [tile] + [_resident(a.shape) for a in args[1:]]
    return pl.pallas_call(
        _s5_kernel,
        grid=(bsz, seq // tl),
        in_specs=in_specs,
        out_specs=tile,
        out_shape=jax.ShapeDtypeStruct((bsz, seq, width), BF16),
        scratch_shapes=[pltpu.VMEM((tl, 2 * n_state), F32),
                        pltpu.VMEM((1, n_state), F32), pltpu.VMEM((1, n_state), F32)],
        compiler_params=_params("parallel", "arbitrary"),
        name="s5",
    )(*args)


def _rel_bias_tiles(rel_bias, tile):
    n_buckets = rel_bias.shape[0]
    max_exact = n_buckets // 2

    def bucket(rel):
        n = np.maximum(rel, 0)
        n_f = np.maximum(n, 1).astype(np.float32)
        large = max_exact + (np.log(n_f / np.float32(max_exact)) / np.float32(math.log(MAX_DISTANCE / max_exact))
                             * np.float32(n_buckets - max_exact)).astype(np.int32)
        return np.where(n < max_exact, n, np.minimum(large, n_buckets - 1))

    assert bucket(np.array([tile + 1]))[0] == n_buckets - 1, "far key tiles must share one bucket"
    rel0 = np.arange(tile)[:, None] - np.arange(tile)[None, :]
    table = rel_bias.astype(F32)
    shifted = table - table[n_buckets - 1][None, :]
    diag = jnp.where(jnp.asarray(rel0 >= 0)[:, :, None], shifted[bucket(rel0)], -jnp.inf)
    sub = shifted[bucket(rel0 + tile)]
    tiles = jnp.stack([diag, sub, jnp.zeros_like(sub)], axis=0)
    return tiles.transpose(3, 0, 1, 2)


def _attn_kernel(q_ref, k_ref, v_ref, bias_ref, lp_ref, sub_ref, o_ref, *, lam_init):
    tq = q_ref.shape[1]
    dv = q_ref.shape[2]
    dqk = dv // 2
    i = pl.program_id(2)
    q = q_ref[0]
    q1, q2 = q[:, :dqk], q[:, dqk:]

    def update(s, m, l, acc, vt):
        m_new = jnp.maximum(m, jnp.max(s, axis=-1, keepdims=True))
        alpha = jnp.exp(m - m_new)
        p = jnp.exp(s - m_new)
        l = alpha * l + jnp.sum(p, axis=-1, keepdims=True)
        acc = alpha * acc + _dot(p.astype(BF16), vt)
        return m_new, l, acc

    def body(kj, carry):
        m1, l1, a1, m2, l2, a2 = carry
        r = pl.multiple_of(kj * tq, tq)
        kt = k_ref[0, pl.ds(r, tq), :]
        vt = v_ref[0, pl.ds(r, tq), :]
        bias = bias_ref[0, jnp.minimum(i - kj, 2)]
        m1, l1, a1 = update(_dot_nt(q1, kt[:, :dqk]) + bias, m1, l1, a1, vt)
        m2, l2, a2 = update(_dot_nt(q2, kt[:, dqk:]) + bias, m2, l2, a2, vt)
        return m1, l1, a1, m2, l2, a2

    neg = jnp.full((tq, 1), -jnp.inf, F32)
    zl = jnp.zeros((tq, 1), F32)
    za = jnp.zeros((tq, dv), F32)
    _, l1, a1, _, l2, a2 = lax.fori_loop(0, i + 1, body, (neg, zl, za, neg, zl, za))

    lp = lp_ref[...]
    lam = (jnp.exp(jnp.sum(lp[0:1] * lp[1:2], axis=-1, keepdims=True))
           - jnp.exp(jnp.sum(lp[2:3] * lp[3:4], axis=-1, keepdims=True)) + lam_init)
    o = a1 / l1 - lam * (a2 / l2)
    o_ref[0] = (_rms(o, sub_ref[...]) * (1.0 - lam_init)).astype(BF16)


def _diff_attention(q, k, v, rel_bias, lam_params, subln, lam_init):
    bsz, seq, width = q.shape
    dv = width // DIFF_HEADS
    tq = min(ATTN_TILE, seq)
    bias = _rel_bias_tiles(rel_bias, tq)
    q_spec = pl.BlockSpec((1, tq, dv), lambda b, h, i: (b, i, h))
    kv_spec = pl.BlockSpec((1, seq, dv), lambda b, h, i: (b, 0, h))
    return pl.pallas_call(
        functools.partial(_attn_kernel, lam_init=lam_init),
        grid=(bsz, DIFF_HEADS, seq // tq),
        in_specs=[q_spec, kv_spec, kv_spec,
                  pl.BlockSpec((1, 3, tq, tq), lambda b, h, i: (h, 0, 0, 0)),
                  _resident(lam_params.shape), _resident((1, dv))],
        out_specs=q_spec,
        out_shape=jax.ShapeDtypeStruct((bsz, seq, width), BF16),
        compiler_params=_params("parallel", "parallel", "parallel"),
        name="diff_attn",
    )(q, k, v, bias, lam_params, subln.reshape(1, dv))


def _memkv_kernel(m_ref, g_ref, w_ref, k_ref, v_ref):
    d = m_ref.shape[2]
    n = _rms(m_ref[0], g_ref[...]).astype(BF16)
    k_ref[0] = _dot(n, w_ref[:, :d]).astype(BF16)
    v_ref[0] = _dot(n, w_ref[:, d:]).astype(BF16)


def _memkv(mem, gain, wkv):
    bsz, m_len, d = mem.shape
    blk = pl.BlockSpec((1, m_len, d), lambda b: (b, 0, 0))
    return pl.pallas_call(
        _memkv_kernel,
        grid=(bsz,),
        in_specs=[blk, _resident((1, d)), _resident((d, 2 * d))],
        out_specs=[blk, blk],
        out_shape=[jax.ShapeDtypeStruct((bsz, m_len, d), BF16)] * 2,
        compiler_params=_params("parallel"),
        name="memkv",
    )(mem, gain.reshape(1, d), wkv.astype(BF16))


def _mixout_xattn_kernel(h_ref, ys_ref, ya_ref, wout_ref, g_ref, wq_ref, km_ref, vm_ref, wo_ref,
                         o_ref, oh_ref, *, q_scale):
    h2 = h_ref[0] + _dot(jnp.concatenate([ys_ref[0], ya_ref[0]], axis=1), wout_ref[...])
    n = _rms(h2, g_ref[...]).astype(BF16)
    q = (_dot(n, wq_ref[...]) * q_scale).astype(BF16)
    d = q.shape[1]
    hd = d // XATTN_HEADS
    for hh in range(XATTN_HEADS):
        sl = slice(hh * hd, (hh + 1) * hd)
        s = _dot_nt(q[:, sl], km_ref[0, :, sl])
        p = jnp.exp(s - jnp.max(s, axis=-1, keepdims=True))
        o = _dot(p.astype(BF16), vm_ref[0, :, sl]) / jnp.sum(p, axis=-1, keepdims=True)
        oh_ref[:, sl] = o.astype(BF16)
    o_ref[0] = h2 + _dot(oh_ref[...], wo_ref[...])


def _mixout_xattn(h, y_ssm, y_attn, w_out, gain, wq, k_mem, v_mem, wo):
    bsz, seq, d = h.shape
    ws, wa = y_ssm.shape[2], y_attn.shape[2]
    m_len = k_mem.shape[1]
    tm = min(TOKEN_TILE, seq)
    tok = lambda w: pl.BlockSpec((1, tm, w), lambda b, i: (b, i, 0))
    mem = pl.BlockSpec((1, m_len, d), lambda b, i: (b, 0, 0))
    return pl.pallas_call(
        functools.partial(_mixout_xattn_kernel, q_scale=(d // XATTN_HEADS) ** -0.5),
        grid=(bsz, seq // tm),
        in_specs=[tok(d), tok(ws), tok(wa), _resident((ws + wa, d)), _resident((1, d)),
                  _resident((d, d)), mem, mem, _resident((d, d))],
        out_specs=tok(d),
        out_shape=jax.ShapeDtypeStruct((bsz, seq, d), F32),
        scratch_shapes=[pltpu.VMEM((tm, d), BF16)],
        compiler_params=_params("parallel", "parallel"),
        name="mixout_xattn",
    )(h, y_ssm, y_attn, w_out.astype(BF16), gain.reshape(1, d), wq.astype(BF16), k_mem, v_mem,
      wo.astype(BF16))


def kernel(x, mem, rel_bias, ffn1_norm, ffn1_w_gate, ffn1_w_up, ffn1_w_down, mix_norm, w_in,
           ssm_lambda_re, ssm_lambda_im, ssm_b_re, ssm_b_im, ssm_c_re, ssm_c_im, ssm_d,
           ssm_log_dt, ssm_w_glu, ssm_b_glu, ssm_out_norm, diff_lambda_q1, diff_lambda_k1,
           diff_lambda_q2, diff_lambda_k2, diff_subln, w_out, xattn_norm, mem_norm, xattn_wq,
           xattn_wkv, xattn_wo, ffn2_norm, ffn2_w_gate, ffn2_w_up, ffn2_w_down, final_norm):
    bsz, seq, d = x.shape
    depth = w_in.shape[0]
    width = w_in.shape[2] // 4
    dqk = width // DIFF_HEADS // 2
    t = bsz * seq
    assert t % TOKEN_TILE == 0 and seq % min(S5_TILE, seq) == 0 and seq % min(ATTN_TILE, seq) == 0
    h = x
    for l in range(depth):
        lam_init = 0.8 - 0.6 * math.exp(-0.3 * l)
        h = _ffn(h.reshape(t, d), ffn1_norm[l], ffn1_w_gate[l], ffn1_w_up[l], ffn1_w_down[l])
        u, q, k, v = _inproj(h, mix_norm[l], w_in[l], width, dqk ** -0.5)
        shp = (bsz, seq, width)
        y_ssm = _s5(u.reshape(shp), ssm_lambda_re[l], ssm_lambda_im[l], ssm_b_re[l], ssm_b_im[l],
                    ssm_c_re[l], ssm_c_im[l], ssm_d[l], ssm_log_dt[l], ssm_w_glu[l], ssm_b_glu[l],
                    ssm_out_norm[l])
        lam_params = jnp.stack([diff_lambda_q1[l], diff_lambda_k1[l], diff_lambda_q2[l], diff_lambda_k2[l]])
        y_attn = _diff_attention(q.reshape(shp), k.reshape(shp), v.reshape(shp), rel_bias,
                                 lam_params.astype(F32), diff_subln[l], lam_init)
        k_mem, v_mem = _memkv(mem, mem_norm[l], xattn_wkv[l])
        h = _mixout_xattn(h.reshape(bsz, seq, d), y_ssm, y_attn, w_out[l], xattn_norm[l], xattn_wq[l],
                          k_mem, v_mem, xattn_wo[l])
        h = _ffn(h.reshape(t, d), ffn2_norm[l], ffn2_w_gate[l], ffn2_w_up[l], ffn2_w_down[l],
                 final_gain=final_norm if l == depth - 1 else None)
    return h.reshape(bsz, seq, d)
```

```python
import functools
import math

import numpy as np
import jax
import jax.numpy as jnp
from jax import lax
from jax.experimental import pallas as pl
from jax.experimental.pallas import tpu as pltpu

F32 = jnp.float32
BF16 = jnp.bfloat16

EPS = 1e-6
SSM_GROUP = 16
DIFF_HEADS = 4
XATTN_HEADS = 4
MAX_DISTANCE = 128
SUBLANES = 8
MXU_DIM = 256
VMEM_LIMIT_BYTES = 56 * 1024 * 1024

TOKEN_TILE = 512
FFN_CHUNK = 512
ATTN_TILE = 512
S5_TILE = 512
S5_COLS = 1024


def _rms(x, gain):
    ms = jnp.mean(x * x, axis=-1, keepdims=True)
    return x * lax.rsqrt(ms + EPS) * gain


def _dot(a, b):
    return jnp.dot(a, b, preferred_element_type=F32)


def _dot_nt(a, b):
    return lax.dot_general(a, b, (((1,), (1,)), ((), ())), preferred_element_type=F32)


def _resident(shape):
    zeros = (0,) * len(shape)
    return pl.BlockSpec(shape, lambda *_: zeros, pipeline_mode=pl.Buffered(1))


def _params(*semantics):
    return pltpu.CompilerParams(dimension_semantics=semantics, vmem_limit_bytes=VMEM_LIMIT_BYTES)


def _ffn_kernel(h_ref, g_ref, wg_ref, wu_ref, wd_ref, *rest, final):
    if final:
        fg_ref, o_ref, a_ref = rest
    else:
        o_ref, a_ref = rest
    n = _rms(h_ref[...], g_ref[...]).astype(BF16)
    d_ff = wg_ref.shape[1]
    for c0 in range(0, d_ff, FFN_CHUNK):
        c1 = min(c0 + FFN_CHUNK, d_ff)
        g = _dot(n, wg_ref[:, c0:c1])
        u = _dot(n, wu_ref[:, c0:c1])
        a_ref[:, c0:c1] = (jax.nn.silu(g) * u).astype(BF16)
    out = h_ref[...] + 0.5 * _dot(a_ref[...], wd_ref[...])
    if final:
        out = _rms(out, fg_ref[...])
    o_ref[...] = out


def _ffn(h, gain, w_gate, w_up, w_down, final_gain=None):
    t, d = h.shape
    d_ff = w_gate.shape[1]
    tm = TOKEN_TILE
    final = final_gain is not None
    row = pl.BlockSpec((tm, d), lambda i: (i, 0))
    in_specs = [row, _resident((1, d)), _resident((d, d_ff)), _resident((d, d_ff)), _resident((d_ff, d))]
    args = [h, gain.reshape(1, d), w_gate.astype(BF16), w_up.astype(BF16), w_down.astype(BF16)]
    if final:
        in_specs.append(_resident((1, d)))
        args.append(final_gain.reshape(1, d))
    return pl.pallas_call(
        functools.partial(_ffn_kernel, final=final),
        grid=(t // tm,),
        in_specs=in_specs,
        out_specs=row,
        out_shape=jax.ShapeDtypeStruct((t, d), F32),
        scratch_shapes=[pltpu.VMEM((tm, d_ff), BF16)],
        compiler_params=_params("parallel"),
        name="ffn_final" if final else "ffn",
    )(*args)


def _inproj_kernel(h_ref, g_ref, w_ref, u_ref, q_ref, k_ref, v_ref, *, q_scale):
    n = _rms(h_ref[...], g_ref[...]).astype(BF16)
    w = u_ref.shape[1]
    u_ref[...] = _dot(n, w_ref[:, 0:w])
    q_ref[...] = (_dot(n, w_ref[:, w:2 * w]) * q_scale).astype(BF16)
    k_ref[...] = _dot(n, w_ref[:, 2 * w:3 * w]).astype(BF16)
    v_ref[...] = _dot(n, w_ref[:, 3 * w:4 * w]).astype(BF16)


def _inproj(h, gain, w_in, width, q_scale):
    t, d = h.shape
    tm = TOKEN_TILE
    assert w_in.shape[1] == 4 * width
    out_spec = pl.BlockSpec((tm, width), lambda i: (i, 0))
    return pl.pallas_call(
        functools.partial(_inproj_kernel, q_scale=q_scale),
        grid=(t // tm,),
        in_specs=[pl.BlockSpec((tm, d), lambda i: (i, 0)), _resident((1, d)), _resident((d, 4 * width))],
        out_specs=[out_spec] * 4,
        out_shape=[jax.ShapeDtypeStruct((t, width), F32)] + [jax.ShapeDtypeStruct((t, width), BF16)] * 3,
        compiler_params=_params("parallel"),
        name="inproj",
    )(h, gain.reshape(1, d), w_in.astype(BF16))


def _s5_kernel(u_ref, perm3_ref, perm_t_ref, bre_ref, bim_ref, lre_ref, lim_ref, ltre_ref, ltim_ref,
               c_ref, d_ref, wglu_ref, bglu_ref, gn_ref, o_ref, x_ref, carry_re_ref, carry_im_ref):
    tl = u_ref.shape[1]
    tc = tl // SUBLANES
    width = u_ref.shape[2]
    n_kb = width // MXU_DIM
    sb = x_ref.shape[1] // (2 * n_kb)

    @pl.when(pl.program_id(1) == 0)
    def _():
        carry_re_ref[...] = jnp.zeros_like(carry_re_ref)
        carry_im_ref[...] = jnp.zeros_like(carry_im_ref)

    u = u_ref[0]
    u_hi = u.astype(BF16)
    rem = u - u_hi.astype(F32)
    u_mid = rem.astype(BF16)
    u_lo = (rem - u_mid.astype(F32)).astype(BF16)
    up = _dot(perm3_ref[...], jnp.concatenate([u_hi, u_mid, u_lo], axis=0))
    up_b = _dot(perm3_ref[:, :tl], u_hi).astype(BF16)

    for kb in range(n_kb):
        ub = up_b[:, kb * MXU_DIM:(kb + 1) * MXU_DIM]
        x_ref[:, 2 * kb * sb:(2 * kb + 1) * sb] = _dot(ub, bre_ref[kb])
        x_ref[:, (2 * kb + 1) * sb:(2 * kb + 2) * sb] = _dot(ub, bim_ref[kb])

    for kb in range(n_kb):
        for c0 in range(0, sb, S5_COLS):
            lam_cols = slice(kb * sb + c0, kb * sb + c0 + S5_COLS)
            re_cols = slice(2 * kb * sb + c0, 2 * kb * sb + c0 + S5_COLS)
            im_cols = slice((2 * kb + 1) * sb + c0, (2 * kb + 1) * sb + c0 + S5_COLS)
            lr = jnp.broadcast_to(lre_ref[:, lam_cols], (SUBLANES, S5_COLS))
            li = jnp.broadcast_to(lim_ref[:, lam_cols], (SUBLANES, S5_COLS))

            def step(j, xr, xi):
                r = pl.multiple_of(j * SUBLANES, SUBLANES)
                br = x_ref[pl.ds(r, SUBLANES), re_cols]
                bi = x_ref[pl.ds(r, SUBLANES), im_cols]
                return r, lr * xr - li * xi + br, lr * xi + li * xr + bi

            def end_state_body(j, x):
                _, nr, ni = step(j, *x)
                return nr, ni

            zero = jnp.zeros((SUBLANES, S5_COLS), F32)
            end_re, end_im = lax.fori_loop(0, tc, end_state_body, (zero, zero), unroll=4)

            ltr = ltre_ref[:, lam_cols]
            lti = ltim_ref[:, lam_cols]
            cr = carry_re_ref[:, lam_cols]
            ci = carry_im_ref[:, lam_cols]
            init_re, init_im = [], []
            for k in range(SUBLANES):
                init_re.append(cr)
                init_im.append(ci)
                cr, ci = (ltr * cr - lti * ci + end_re[k:k + 1], ltr * ci + lti * cr + end_im[k:k + 1])
            carry_re_ref[:, lam_cols] = cr
            carry_im_ref[:, lam_cols] = ci

            def store_body(j, x):
                r, nr, ni = step(j, *x)
                x_ref[pl.ds(r, SUBLANES), re_cols] = nr
                x_ref[pl.ds(r, SUBLANES), im_cols] = ni
                return nr, ni

            lax.fori_loop(0, tc, store_body,
                          (jnp.concatenate(init_re, axis=0), jnp.concatenate(init_im, axis=0)), unroll=4)

    parts = [_dot(x_ref[:, 2 * kb * sb:(2 * kb + 2) * sb].astype(BF16), c_ref[kb]) for kb in range(n_kb)]
    y = jnp.concatenate(parts, axis=1) + d_ref[...] * up
    z = _dot(jax.nn.gelu(y).astype(BF16), wglu_ref[...]) + bglu_ref[...]
    gated = z[:, :width] * jax.nn.sigmoid(z[:, width:])
    out = _rms(gated, gn_ref[...]).astype(BF16)
    o_ref[0] = _dot(perm_t_ref[...], out).astype(BF16)


def _s5_discretise(lam_re, lam_im, b_re, b_im, log_dt, n_sub):
    a = jnp.minimum(lam_re, -1e-4)
    dt = jnp.exp(log_dt)[:, None]
    za, zb = a * dt, lam_im * dt
    ea = jnp.exp(za)
    bar_re, bar_im = ea * jnp.cos(zb), ea * jnp.sin(zb)
    num_re = jnp.expm1(za) * jnp.cos(zb) - 2.0 * jnp.sin(0.5 * zb) ** 2
    num_im = bar_im
    den = a * a + lam_im * lam_im
    f_re = (num_re * a + num_im * lam_im) / den
    f_im = (num_im * a - num_re * lam_im) / den
    bb_re = f_re[:, :, None] * b_re - f_im[:, :, None] * b_im
    bb_im = f_re[:, :, None] * b_im + f_im[:, :, None] * b_re
    en = jnp.exp(za * n_sub)
    pow_re, pow_im = en * jnp.cos(zb * n_sub), en * jnp.sin(zb * n_sub)
    return bar_re, bar_im, bb_re, bb_im, pow_re, pow_im


def _s5(u, lam_re, lam_im, b_re, b_im, c_re, c_im, d, log_dt, w_glu, b_glu, out_norm):
    bsz, seq, width = u.shape
    n_groups, n_p = lam_re.shape
    gh = SSM_GROUP
    assert n_groups * gh == width and width % MXU_DIM == 0
    tl = min(S5_TILE, seq)
    tc = tl // SUBLANES
    n_state = n_groups * n_p
    n_kb = width // MXU_DIM
    gpb = n_groups // n_kb
    assert (gpb * n_p) % S5_COLS == 0
    bar_re, bar_im, bb_re, bb_im, pow_re, pow_im = _s5_discretise(lam_re, lam_im, b_re, b_im, log_dt, tc)

    eye = jnp.eye(gpb, dtype=F32)

    def in_blocks(bb):
        bb = bb.reshape(n_kb, gpb, n_p, gh)
        return jnp.einsum('kgph,gj->kghjp', bb, eye).reshape(n_kb, gpb * gh, gpb * n_p).astype(BF16)

    def out_blocks(c):
        c = c.reshape(n_kb, gpb, gh, n_p)
        return jnp.einsum('kghp,gj->kgpjh', c, eye).reshape(n_kb, gpb * n_p, gpb * gh).astype(BF16)

    rows = np.arange(tl)
    perm_np = np.zeros((tl, tl), np.float32)
    perm_np[rows, (rows % SUBLANES) * tc + rows // SUBLANES] = 1.0
    perm3 = jnp.asarray(np.concatenate([perm_np] * 3, axis=1), BF16)
    perm_t = jnp.asarray(perm_np.T, BF16)

    flat = lambda x: x.reshape(1, n_state)
    c_blocks = jnp.concatenate([out_blocks(c_re), out_blocks(-c_im)], axis=1)
    args = [u, perm3, perm_t, in_blocks(bb_re), in_blocks(bb_im), flat(bar_re), flat(bar_im),
            flat(pow_re), flat(pow_im), c_blocks, d.reshape(1, width),
            w_glu.astype(BF16), b_glu.reshape(1, 2 * width), out_norm.reshape(1, width)]
    tile = pl.BlockSpec((1, tl, width), lambda b, l: (b, l, 0))
    in_specs = [tile] + [_resident(a.shape) for a in args[1:]]
    return pl.pallas_call(
        _s5_kernel,
        grid=(bsz, seq // tl),
        in_specs=in_specs,
        out_specs=tile,
        out_shape=jax.ShapeDtypeStruct((bsz, seq, width), BF16),
        scratch_shapes=[pltpu.VMEM((tl, 2 * n_state), F32),
                        pltpu.VMEM((1, n_state), F32), pltpu.VMEM((1, n_state), F32)],
        compiler_params=_params("parallel", "arbitrary"),
        name="s5",
    )(*args)


def _rel_bias_tiles(rel_bias, tile):
    n_buckets = rel_bias.shape[0]
    max_exact = n_buckets // 2

    def bucket(rel):
        n = np.maximum(rel, 0)
        n_f = np.maximum(n, 1).astype(np.float32)
        large = max_exact + (np.log(n_f / np.float32(max_exact)) / np.float32(math.log(MAX_DISTANCE / max_exact))
                             * np.float32(n_buckets - max_exact)).astype(np.int32)
        return np.where(n < max_exact, n, np.minimum(large, n_buckets - 1))

    assert bucket(np.array([tile + 1]))[0] == n_buckets - 1, "far key tiles must share one bucket"
    rel0 = np.arange(tile)[:, None] - np.arange(tile)[None, :]
    table = rel_bias.astype(F32)
    shifted = (table - table[n_buckets - 1][None, :]).T

    def lookup(rel):
        onehot = (jnp.asarray(bucket(rel).reshape(-1))[None, :] == jnp.arange(n_buckets)[:, None]).astype(F32)
        return jnp.dot(shifted, onehot, precision=lax.Precision.HIGHEST).reshape(-1, tile, tile)

    diag = jnp.where(jnp.asarray(rel0 >= 0)[None], lookup(rel0), -jnp.inf)
    sub = lookup(rel0 + tile)
    return jnp.stack([diag, sub, jnp.zeros_like(sub)], axis=1)


def _attn_kernel(q_ref, k_ref, v_ref, bias_ref, lp_ref, sub_ref, o_ref, *, lam_init):
    tq = q_ref.shape[1]
    dv = q_ref.shape[2]
    dqk = dv // 2
    i = pl.program_id(2)
    q = q_ref[0]
    q1, q2 = q[:, :dqk], q[:, dqk:]

    def update(s, m, l, acc, vt):
        m_new = jnp.maximum(m, jnp.max(s, axis=-1, keepdims=True))
        alpha = jnp.exp(m - m_new)
        p = jnp.exp(s - m_new)
        l = alpha * l + jnp.sum(p, axis=-1, keepdims=True)
        acc = alpha * acc + _dot(p.astype(BF16), vt)
        return m_new, l, acc

    def body(kj, carry):
        m1, l1, a1, m2, l2, a2 = carry
        r = pl.multiple_of(kj * tq, tq)
        kt = k_ref[0, pl.ds(r, tq), :]
        vt = v_ref[0, pl.ds(r, tq), :]
        bias = bias_ref[0, jnp.minimum(i - kj, 2)]
        m1, l1, a1 = update(_dot_nt(q1, kt[:, :dqk]) + bias, m1, l1, a1, vt)
        m2, l2, a2 = update(_dot_nt(q2, kt[:, dqk:]) + bias, m2, l2, a2, vt)
        return m1, l1, a1, m2, l2, a2

    neg = jnp.full((tq, 1), -jnp.inf, F32)
    zl = jnp.zeros((tq, 1), F32)
    za = jnp.zeros((tq, dv), F32)
    _, l1, a1, _, l2, a2 = lax.fori_loop(0, i + 1, body, (neg, zl, za, neg, zl, za))

    lp = lp_ref[...]
    lam = (jnp.exp(jnp.sum(lp[0:1] * lp[1:2], axis=-1, keepdims=True))
           - jnp.exp(jnp.sum(lp[2:3] * lp[3:4], axis=-1, keepdims=True)) + lam_init)
    o = a1 / l1 - lam * (a2 / l2)
    o_ref[0] = (_rms(o, sub_ref[...]) * (1.0 - lam_init)).astype(BF16)


def _diff_attention(q, k, v, bias, lam_params, subln, lam_init):
    bsz, seq, width = q.shape
    dv = width // DIFF_HEADS
    tq = bias.shape[2]
    q_spec = pl.BlockSpec((1, tq, dv), lambda b, h, i: (b, i, h))
    kv_spec = pl.BlockSpec((1, seq, dv), lambda b, h, i: (b, 0, h))
    return pl.pallas_call(
        functools.partial(_attn_kernel, lam_init=lam_init),
        grid=(bsz, DIFF_HEADS, seq // tq),
        in_specs=[q_spec, kv_spec, kv_spec,
                  pl.BlockSpec((1, 3, tq, tq), lambda b, h, i: (h, 0, 0, 0)),
                  _resident(lam_params.shape), _resident((1, dv))],
        out_specs=q_spec,
        out_shape=jax.ShapeDtypeStruct((bsz, seq, width), BF16),
        compiler_params=_params("parallel", "parallel", "parallel"),
        name="diff_attn",
    )(q, k, v, bias, lam_params, subln.reshape(1, dv))


def _memkv_kernel(m_ref, g_ref, w_ref, k_ref, v_ref):
    d = m_ref.shape[2]
    n = _rms(m_ref[0], g_ref[...]).astype(BF16)
    k_ref[0] = _dot(n, w_ref[:, :d]).astype(BF16)
    v_ref[0] = _dot(n, w_ref[:, d:]).astype(BF16)


def _memkv(mem, gain, wkv):
    bsz, m_len, d = mem.shape
    blk = pl.BlockSpec((1, m_len, d), lambda b: (b, 0, 0))
    return pl.pallas_call(
        _memkv_kernel,
        grid=(bsz,),
        in_specs=[blk, _resident((1, d)), _resident((d, 2 * d))],
        out_specs=[blk, blk],
        out_shape=[jax.ShapeDtypeStruct((bsz, m_len, d), BF16)] * 2,
        compiler_params=_params("parallel"),
        name="memkv",
    )(mem, gain.reshape(1, d), wkv.astype(BF16))


def _mixout_xattn_kernel(h_ref, ys_ref, ya_ref, wout_ref, g_ref, wq_ref, km_ref, vm_ref, wo_ref,
                         o_ref, oh_ref, *, q_scale):
    h2 = h_ref[0] + _dot(jnp.concatenate([ys_ref[0], ya_ref[0]], axis=1), wout_ref[...])
    n = _rms(h2, g_ref[...]).astype(BF16)
    q = (_dot(n, wq_ref[...]) * q_scale).astype(BF16)
    d = q.shape[1]
    hd = d // XATTN_HEADS
    for hh in range(XATTN_HEADS):
        sl = slice(hh * hd, (hh + 1) * hd)
        s = _dot_nt(q[:, sl], km_ref[0, :, sl])
        p = jnp.exp(s - jnp.max(s, axis=-1, keepdims=True))
        o = _dot(p.astype(BF16), vm_ref[0, :, sl]) / jnp.sum(p, axis=-1, keepdims=True)
        oh_ref[:, sl] = o.astype(BF16)
    o_ref[0] = h2 + _dot(oh_ref[...], wo_ref[...])


def _mixout_xattn(h, y_ssm, y_attn, w_out, gain, wq, k_mem, v_mem, wo):
    bsz, seq, d = h.shape
    ws, wa = y_ssm.shape[2], y_attn.shape[2]
    m_len = k_mem.shape[1]
    tm = min(TOKEN_TILE, seq)
    tok = lambda w: pl.BlockSpec((1, tm, w), lambda b, i: (b, i, 0))
    mem = pl.BlockSpec((1, m_len, d), lambda b, i: (b, 0, 0))
    return pl.pallas_call(
        functools.partial(_mixout_xattn_kernel, q_scale=(d // XATTN_HEADS) ** -0.5),
        grid=(bsz, seq // tm),
        in_specs=[tok(d), tok(ws), tok(wa), _resident((ws + wa, d)), _resident((1, d)),
                  _resident((d, d)), mem, mem, _resident((d, d))],
        out_specs=tok(d),
        out_shape=jax.ShapeDtypeStruct((bsz, seq, d), F32),
        scratch_shapes=[pltpu.VMEM((tm, d), BF16)],
        compiler_params=_params("parallel", "parallel"),
        name="mixout_xattn",
    )(h, y_ssm, y_attn, w_out.astype(BF16), gain.reshape(1, d), wq.astype(BF16), k_mem, v_mem,
      wo.astype(BF16))


def kernel(x, mem, rel_bias, ffn1_norm, ffn1_w_gate, ffn1_w_up, ffn1_w_down, mix_norm, w_in,
           ssm_lambda_re, ssm_lambda_im, ssm_b_re, ssm_b_im, ssm_c_re, ssm_c_im, ssm_d,
           ssm_log_dt, ssm_w_glu, ssm_b_glu, ssm_out_norm, diff_lambda_q1, diff_lambda_k1,
           diff_lambda_q2, diff_lambda_k2, diff_subln, w_out, xattn_norm, mem_norm, xattn_wq,
           xattn_wkv, xattn_wo, ffn2_norm, ffn2_w_gate, ffn2_w_up, ffn2_w_down, final_norm):
    bsz, seq, d = x.shape
    depth = w_in.shape[0]
    width = w_in.shape[2] // 4
    dqk = width // DIFF_HEADS // 2
    t = bsz * seq
    assert t % TOKEN_TILE == 0 and seq % min(S5_TILE, seq) == 0 and seq % min(ATTN_TILE, seq) == 0
    h = x
    bias_tiles = _rel_bias_tiles(rel_bias, min(ATTN_TILE, seq))
    for l in range(depth):
        lam_init = 0.8 - 0.6 * math.exp(-0.3 * l)
        h = _ffn(h.reshape(t, d), ffn1_norm[l], ffn1_w_gate[l], ffn1_w_up[l], ffn1_w_down[l])
        u, q, k, v = _inproj(h, mix_norm[l], w_in[l], width, dqk ** -0.5)
        shp = (bsz, seq, width)
        y_ssm = _s5(u.reshape(shp), ssm_lambda_re[l], ssm_lambda_im[l], ssm_b_re[l], ssm_b_im[l],
                    ssm_c_re[l], ssm_c_im[l], ssm_d[l], ssm_log_dt[l], ssm_w_glu[l], ssm_b_glu[l],
                    ssm_out_norm[l])
        lam_params = jnp.stack([diff_lambda_q1[l], diff_lambda_k1[l], diff_lambda_q2[l], diff_lambda_k2[l]])
        y_attn = _diff_attention(q.reshape(shp), k.reshape(shp), v.reshape(shp), bias_tiles,
                                 lam_params.astype(F32), diff_subln[l], lam_init)
        k_mem, v_mem = _memkv(mem, mem_norm[l], xattn_wkv[l])
        h = _mixout_xattn(h.reshape(bsz, seq, d), y_ssm, y_attn, w_out[l], xattn_norm[l], xattn_wq[l],
                          k_mem, v_mem, xattn_wo[l])
        h = _ffn(h.reshape(t, d), ffn2_norm[l], ffn2_w_gate[l], ffn2_w_up[l], ffn2_w_down[l],
                 final_gain=final_norm if l == depth - 1 else None)
    return h.reshape(bsz, seq, d)
```

```python
import functools
import math

import numpy as np
import jax
import jax.numpy as jnp
from jax import lax
from jax.experimental import pallas as pl
from jax.experimental.pallas import tpu as pltpu

F32 = jnp.float32
BF16 = jnp.bfloat16

EPS = 1e-6
SSM_GROUP = 16
DIFF_HEADS = 4
XATTN_HEADS = 4
MAX_DISTANCE = 128
LOG2E = math.log2(math.e)
MASKED = -1e30
SUBLANES = 8
MXU_DIM = 256
VMEM_LIMIT_BYTES = 56 * 1024 * 1024

TOKEN_TILE = 512
FFN_CHUNK = 512
ATTN_TILE = 512
ATTN_ROWS = 128
ATTN_KEYS = 256
S5_TILE = 512
S5_COLS = 1024


def _rms(x, gain):
    ms = jnp.mean(x * x, axis=-1, keepdims=True)
    return x * lax.rsqrt(ms + EPS) * gain


def _dot(a, b):
    return jnp.dot(a, b, preferred_element_type=F32)


def _dot_nt(a, b):
    return lax.dot_general(a, b, (((1,), (1,)), ((), ())), preferred_element_type=F32)


def _resident(shape):
    zeros = (0,) * len(shape)
    return pl.BlockSpec(shape, lambda *_: zeros, pipeline_mode=pl.Buffered(1))


def _params(*semantics):
    return pltpu.CompilerParams(dimension_semantics=semantics, vmem_limit_bytes=VMEM_LIMIT_BYTES)


def _ffn_kernel(h_ref, g_ref, wg_ref, wu_ref, wd_ref, *rest, final):
    if final:
        fg_ref, o_ref, a_ref = rest
    else:
        o_ref, a_ref = rest
    n = _rms(h_ref[...], g_ref[...]).astype(BF16)
    d_ff = wg_ref.shape[1]
    for c0 in range(0, d_ff, FFN_CHUNK):
        c1 = min(c0 + FFN_CHUNK, d_ff)
        g = _dot(n, wg_ref[:, c0:c1])
        u = _dot(n, wu_ref[:, c0:c1])
        a_ref[:, c0:c1] = (jax.nn.silu(g) * u).astype(BF16)
    out = h_ref[...] + 0.5 * _dot(a_ref[...], wd_ref[...])
    if final:
        out = _rms(out, fg_ref[...])
    o_ref[...] = out


def _ffn(h, gain, w_gate, w_up, w_down, final_gain=None):
    t, d = h.shape
    d_ff = w_gate.shape[1]
    tm = TOKEN_TILE
    final = final_gain is not None
    row = pl.BlockSpec((tm, d), lambda i: (i, 0))
    in_specs = [row, _resident((1, d)), _resident((d, d_ff)), _resident((d, d_ff)), _resident((d_ff, d))]
    args = [h, gain.reshape(1, d), w_gate.astype(BF16), w_up.astype(BF16), w_down.astype(BF16)]
    if final:
        in_specs.append(_resident((1, d)))
        args.append(final_gain.reshape(1, d))
    return pl.pallas_call(
        functools.partial(_ffn_kernel, final=final),
        grid=(t // tm,),
        in_specs=in_specs,
        out_specs=row,
        out_shape=jax.ShapeDtypeStruct((t, d), F32),
        scratch_shapes=[pltpu.VMEM((tm, d_ff), BF16)],
        compiler_params=_params("parallel"),
        name="ffn_final" if final else "ffn",
    )(*args)


def _inproj_kernel(h_ref, g_ref, w_ref, u_ref, q_ref, k_ref, v_ref, *, q_scale):
    n = _rms(h_ref[...], g_ref[...]).astype(BF16)
    w = u_ref.shape[1]
    u_ref[...] = _dot(n, w_ref[:, 0:w])
    q_ref[...] = (_dot(n, w_ref[:, w:2 * w]) * q_scale).astype(BF16)
    k_ref[...] = _dot(n, w_ref[:, 2 * w:3 * w]).astype(BF16)
    v_ref[...] = _dot(n, w_ref[:, 3 * w:4 * w]).astype(BF16)


def _inproj(h, gain, w_in, width, q_scale):
    t, d = h.shape
    tm = TOKEN_TILE
    assert w_in.shape[1] == 4 * width
    out_spec = pl.BlockSpec((tm, width), lambda i: (i, 0))
    return pl.pallas_call(
        functools.partial(_inproj_kernel, q_scale=q_scale),
        grid=(t // tm,),
        in_specs=[pl.BlockSpec((tm, d), lambda i: (i, 0)), _resident((1, d)), _resident((d, 4 * width))],
        out_specs=[out_spec] * 4,
        out_shape=[jax.ShapeDtypeStruct((t, width), F32)] + [jax.ShapeDtypeStruct((t, width), BF16)] * 3,
        compiler_params=_params("parallel"),
        name="inproj",
    )(h, gain.reshape(1, d), w_in.astype(BF16))


def _s5_kernel(u_ref, perm3_ref, perm_t_ref, bre_ref, bim_ref, lre_ref, lim_ref, ltre_ref, ltim_ref,
               c_ref, d_ref, wglu_ref, bglu_ref, gn_ref, o_ref, x_ref, carry_re_ref, carry_im_ref):
    tl = u_ref.shape[1]
    tc = tl // SUBLANES
    width = u_ref.shape[2]
    n_kb = width // MXU_DIM
    sb = x_ref.shape[1] // (2 * n_kb)

    @pl.when(pl.program_id(1) == 0)
    def _():
        carry_re_ref[...] = jnp.zeros_like(carry_re_ref)
        carry_im_ref[...] = jnp.zeros_like(carry_im_ref)

    u = u_ref[0]
    u_hi = u.astype(BF16)
    rem = u - u_hi.astype(F32)
    u_mid = rem.astype(BF16)
    u_lo = (rem - u_mid.astype(F32)).astype(BF16)
    up = _dot(perm3_ref[...], jnp.concatenate([u_hi, u_mid, u_lo], axis=0))
    up_b = _dot(perm3_ref[:, :tl], u_hi).astype(BF16)

    for kb in range(n_kb):
        ub = up_b[:, kb * MXU_DIM:(kb + 1) * MXU_DIM]
        x_ref[:, 2 * kb * sb:(2 * kb + 1) * sb] = _dot(ub, bre_ref[kb])
        x_ref[:, (2 * kb + 1) * sb:(2 * kb + 2) * sb] = _dot(ub, bim_ref[kb])

    for kb in range(n_kb):
        for c0 in range(0, sb, S5_COLS):
            lam_cols = slice(kb * sb + c0, kb * sb + c0 + S5_COLS)
            re_cols = slice(2 * kb * sb + c0, 2 * kb * sb + c0 + S5_COLS)
            im_cols = slice((2 * kb + 1) * sb + c0, (2 * kb + 1) * sb + c0 + S5_COLS)
            lr = jnp.broadcast_to(lre_ref[:, lam_cols], (SUBLANES, S5_COLS))
            li = jnp.broadcast_to(lim_ref[:, lam_cols], (SUBLANES, S5_COLS))

            def step(j, xr, xi):
                r = pl.multiple_of(j * SUBLANES, SUBLANES)
                br = x_ref[pl.ds(r, SUBLANES), re_cols]
                bi = x_ref[pl.ds(r, SUBLANES), im_cols]
                return r, lr * xr - li * xi + br, lr * xi + li * xr + bi

            def end_state_body(j, x):
                _, nr, ni = step(j, *x)
                return nr, ni

            zero = jnp.zeros((SUBLANES, S5_COLS), F32)
            end_re, end_im = lax.fori_loop(0, tc, end_state_body, (zero, zero), unroll=4)

            ltr = ltre_ref[:, lam_cols]
            lti = ltim_ref[:, lam_cols]
            cr = carry_re_ref[:, lam_cols]
            ci = carry_im_ref[:, lam_cols]
            init_re, init_im = [], []
            for k in range(SUBLANES):
                init_re.append(cr)
                init_im.append(ci)
                cr, ci = (ltr * cr - lti * ci + end_re[k:k + 1], ltr * ci + lti * cr + end_im[k:k + 1])
            carry_re_ref[:, lam_cols] = cr
            carry_im_ref[:, lam_cols] = ci

            def store_body(j, x):
                r, nr, ni = step(j, *x)
                x_ref[pl.ds(r, SUBLANES), re_cols] = nr
                x_ref[pl.ds(r, SUBLANES), im_cols] = ni
                return nr, ni

            lax.fori_loop(0, tc, store_body,
                          (jnp.concatenate(init_re, axis=0), jnp.concatenate(init_im, axis=0)), unroll=4)

    parts = [_dot(x_ref[:, 2 * kb * sb:(2 * kb + 2) * sb].astype(BF16), c_ref[kb]) for kb in range(n_kb)]
    y = jnp.concatenate(parts, axis=1) + d_ref[...] * up
    z = _dot(jax.nn.gelu(y).astype(BF16), wglu_ref[...]) + bglu_ref[...]
    gated = z[:, :width] * jax.nn.sigmoid(z[:, width:])
    out = _rms(gated, gn_ref[...]).astype(BF16)
    o_ref[0] = _dot(perm_t_ref[...], out).astype(BF16)


def _s5_discretise(lam_re, lam_im, b_re, b_im, log_dt, n_sub):
    a = jnp.minimum(lam_re, -1e-4)
    dt = jnp.exp(log_dt)[:, None]
    za, zb = a * dt, lam_im * dt
    ea = jnp.exp(za)
    bar_re, bar_im = ea * jnp.cos(zb), ea * jnp.sin(zb)
    num_re = jnp.expm1(za) * jnp.cos(zb) - 2.0 * jnp.sin(0.5 * zb) ** 2
    num_im = bar_im
    den = a * a + lam_im * lam_im
    f_re = (num_re * a + num_im * lam_im) / den
    f_im = (num_im * a - num_re * lam_im) / den
    bb_re = f_re[:, :, None] * b_re - f_im[:, :, None] * b_im
    bb_im = f_re[:, :, None] * b_im + f_im[:, :, None] * b_re
    en = jnp.exp(za * n_sub)
    pow_re, pow_im = en * jnp.cos(zb * n_sub), en * jnp.sin(zb * n_sub)
    return bar_re, bar_im, bb_re, bb_im, pow_re, pow_im


def _s5(u, lam_re, lam_im, b_re, b_im, c_re, c_im, d, log_dt, w_glu, b_glu, out_norm):
    bsz, seq, width = u.shape
    n_groups, n_p = lam_re.shape
    gh = SSM_GROUP
    assert n_groups * gh == width and width % MXU_DIM == 0
    tl = min(S5_TILE, seq)
    tc = tl // SUBLANES
    n_state = n_groups * n_p
    n_kb = width // MXU_DIM
    gpb = n_groups // n_kb
    assert (gpb * n_p) % S5_COLS == 0
    bar_re, bar_im, bb_re, bb_im, pow_re, pow_im = _s5_discretise(lam_re, lam_im, b_re, b_im, log_dt, tc)

    eye = jnp.eye(gpb, dtype=F32)

    def in_blocks(bb):
        bb = bb.reshape(n_kb, gpb, n_p, gh)
        return jnp.einsum('kgph,gj->kghjp', bb, eye).reshape(n_kb, gpb * gh, gpb * n_p).astype(BF16)

    def out_blocks(c):
        c = c.reshape(n_kb, gpb, gh, n_p)
        return jnp.einsum('kghp,gj->kgpjh', c, eye).reshape(n_kb, gpb * n_p, gpb * gh).astype(BF16)

    rows = np.arange(tl)
    perm_np = np.zeros((tl, tl), np.float32)
    perm_np[rows, (rows % SUBLANES) * tc + rows // SUBLANES] = 1.0
    perm3 = jnp.asarray(np.concatenate([perm_np] * 3, axis=1), BF16)
    perm_t = jnp.asarray(perm_np.T, BF16)

    flat = lambda x: x.reshape(1, n_state)
    c_blocks = jnp.concatenate([out_blocks(c_re), out_blocks(-c_im)], axis=1)
    args = [u, perm3, perm_t, in_blocks(bb_re), in_blocks(bb_im), flat(bar_re), flat(bar_im),
            flat(pow_re), flat(pow_im), c_blocks, d.reshape(1, width),
            w_glu.astype(BF16), b_glu.reshape(1, 2 * width), out_norm.reshape(1, width)]
    tile = pl.BlockSpec((1, tl, width), lambda b, l: (b, l, 0))
    in_specs = [tile] + [_resident(a.shape) for a in args[1:]]
    return pl.pallas_call(
        _s5_kernel,
        grid=(bsz, seq // tl),
        in_specs=in_specs,
        out_specs=tile,
        out_shape=jax.ShapeDtypeStruct((bsz, seq, width), BF16),
        scratch_shapes=[pltpu.VMEM((tl, 2 * n_state), F32),
                        pltpu.VMEM((1, n_state), F32), pltpu.VMEM((1, n_state), F32)],
        compiler_params=_params("parallel", "arbitrary"),
        name="s5",
    )(*args)


def _rel_bias_tiles(rel_bias, tile):
    n_buckets = rel_bias.shape[0]
    max_exact = n_buckets // 2

    def bucket(rel):
        n = np.maximum(rel, 0)
        n_f = np.maximum(n, 1).astype(np.float32)
        large = max_exact + (np.log(n_f / np.float32(max_exact)) / np.float32(math.log(MAX_DISTANCE / max_exact))
                             * np.float32(n_buckets - max_exact)).astype(np.int32)
        return np.where(n < max_exact, n, np.minimum(large, n_buckets - 1))

    assert bucket(np.array([tile + 1]))[0] == n_buckets - 1, "far key tiles must share one bucket"
    rel0 = np.arange(tile)[:, None] - np.arange(tile)[None, :]
    table = rel_bias.astype(F32)
    shifted = (table - table[n_buckets - 1][None, :]).T * LOG2E

    def lookup(rel):
        onehot = (jnp.asarray(bucket(rel).reshape(-1))[None, :] == jnp.arange(n_buckets)[:, None]).astype(F32)
        return jnp.dot(shifted, onehot, precision=lax.Precision.HIGHEST).reshape(-1, tile, tile)

    diag = jnp.where(jnp.asarray(rel0 >= 0)[None], lookup(rel0), MASKED)
    sub = lookup(rel0 + tile)
    return jnp.stack([diag, sub, jnp.zeros_like(sub), jnp.full_like(sub, MASKED)], axis=1)


def _attn_kernel(q_ref, k_ref, v_ref, bias_ref, lp_ref, sub_ref, o_ref, sa_ref, sb_ref, *, lam_init):
    tq = q_ref.shape[1]
    dv = q_ref.shape[2]
    dqk = dv // 2
    i = pl.program_id(2)
    q = q_ref[0]
    q1, q2 = q[:, :dqk], q[:, dqk:]

    def update(s_ref, mp, state, vt):
        m, l, acc = state
        m_new = jnp.maximum(m, jnp.max(s_ref[mp], axis=-1, keepdims=True))
        alpha = jnp.exp2(m - m_new)
        half = tq // 2
        p = jnp.concatenate([jnp.exp2(s_ref[mp, :, :half] - m_new), jnp.exp2(s_ref[mp, :, half:] - m_new)],
                            axis=1)
        l = alpha * l + jnp.sum(p, axis=-1, keepdims=True)
        acc = alpha * acc + _dot(p.astype(BF16), vt)
        return m_new, l, acc

    def issue_scores(t, s_ref):
        kt = k_ref[0, pl.ds(pl.multiple_of(jnp.minimum(t, i) * tq, tq), tq), :]
        bias = bias_ref[0, jnp.where(t > i, 3, jnp.minimum(i - t, 2))]
        s_ref[0] = _dot_nt(q1, kt[:, :dqk]) + bias
        s_ref[1] = _dot_nt(q2, kt[:, dqk:]) + bias

    def consume(t, s_ref, states):
        vt = v_ref[0, pl.ds(pl.multiple_of(jnp.minimum(t, i) * tq, tq), tq), :]
        return update(s_ref, 0, states[0], vt), update(s_ref, 1, states[1], vt)

    def pair_body(p, states):
        issue_scores(2 * p + 1, sb_ref)
        states = consume(2 * p, sa_ref, states)
        issue_scores(2 * p + 2, sa_ref)
        return consume(2 * p + 1, sb_ref, states)

    init = (jnp.full((tq, 1), -jnp.inf, F32), jnp.zeros((tq, 1), F32), jnp.zeros((tq, dv), F32))
    issue_scores(0, sa_ref)
    (_, l1, a1), (_, l2, a2) = lax.fori_loop(0, (i + 2) // 2, pair_body, (init, init))

    lp = lp_ref[...]
    lam = (jnp.exp(jnp.sum(lp[0:1] * lp[1:2], axis=-1, keepdims=True))
           - jnp.exp(jnp.sum(lp[2:3] * lp[3:4], axis=-1, keepdims=True)) + lam_init)
    o = a1 / l1 - lam * (a2 / l2)
    o_ref[0] = (_rms(o, sub_ref[...]) * (1.0 - lam_init)).astype(BF16)


def _diff_attention(q, k, v, bias, lam_params, subln, lam_init):
    bsz, seq, width = q.shape
    dv = width // DIFF_HEADS
    tq = bias.shape[2]
    q_spec = pl.BlockSpec((1, tq, dv), lambda b, h, i: (b, i, h))
    kv_spec = pl.BlockSpec((1, seq, dv), lambda b, h, i: (b, 0, h))
    return pl.pallas_call(
        functools.partial(_attn_kernel, lam_init=lam_init),
        grid=(bsz, DIFF_HEADS, seq // tq),
        in_specs=[q_spec, kv_spec, kv_spec,
                  pl.BlockSpec((1, 4, tq, tq), lambda b, h, i: (h, 0, 0, 0)),
                  _resident(lam_params.shape), _resident((1, dv))],
        out_specs=q_spec,
        out_shape=jax.ShapeDtypeStruct((bsz, seq, width), BF16),
        scratch_shapes=[pltpu.VMEM((2, tq, tq), F32)] * 2,
        compiler_params=_params("parallel", "parallel", "parallel"),
        name="diff_attn",
    )(q, k, v, bias, lam_params, subln.reshape(1, dv))


def _memkv_kernel(m_ref, g_ref, w_ref, k_ref, v_ref):
    d = m_ref.shape[2]
    n = _rms(m_ref[0], g_ref[...]).astype(BF16)
    k_ref[0] = _dot(n, w_ref[:, :d]).astype(BF16)
    v_ref[0] = _dot(n, w_ref[:, d:]).astype(BF16)


def _memkv(mem, gain, wkv):
    bsz, m_len, d = mem.shape
    blk = pl.BlockSpec((1, m_len, d), lambda b: (b, 0, 0))
    return pl.pallas_call(
        _memkv_kernel,
        grid=(bsz,),
        in_specs=[blk, _resident((1, d)), _resident((d, 2 * d))],
        out_specs=[blk, blk],
        out_shape=[jax.ShapeDtypeStruct((bsz, m_len, d), BF16)] * 2,
        compiler_params=_params("parallel"),
        name="memkv",
    )(mem, gain.reshape(1, d), wkv.astype(BF16))


def _mixout_xattn_kernel(h_ref, ys_ref, ya_ref, wout_ref, g_ref, wq_ref, km_ref, vm_ref, wo_ref,
                         o_ref, oh_ref, *, q_scale):
    h2 = h_ref[0] + _dot(jnp.concatenate([ys_ref[0], ya_ref[0]], axis=1), wout_ref[...])
    n = _rms(h2, g_ref[...]).astype(BF16)
    q = (_dot(n, wq_ref[...]) * q_scale).astype(BF16)
    d = q.shape[1]
    hd = d // XATTN_HEADS
    for hh in range(XATTN_HEADS):
        sl = slice(hh * hd, (hh + 1) * hd)
        s = _dot_nt(q[:, sl], km_ref[0, :, sl])
        p = jnp.exp(s - jnp.max(s, axis=-1, keepdims=True))
        o = _dot(p.astype(BF16), vm_ref[0, :, sl]) / jnp.sum(p, axis=-1, keepdims=True)
        oh_ref[:, sl] = o.astype(BF16)
    o_ref[0] = h2 + _dot(oh_ref[...], wo_ref[...])


def _mixout_xattn(h, y_ssm, y_attn, w_out, gain, wq, k_mem, v_mem, wo):
    bsz, seq, d = h.shape
    ws, wa = y_ssm.shape[2], y_attn.shape[2]
    m_len = k_mem.shape[1]
    tm = min(TOKEN_TILE, seq)
    tok = lambda w: pl.BlockSpec((1, tm, w), lambda b, i: (b, i, 0))
    mem = pl.BlockSpec((1, m_len, d), lambda b, i: (b, 0, 0))
    return pl.pallas_call(
        functools.partial(_mixout_xattn_kernel, q_scale=(d // XATTN_HEADS) ** -0.5),
        grid=(bsz, seq // tm),
        in_specs=[tok(d), tok(ws), tok(wa), _resident((ws + wa, d)), _resident((1, d)),
                  _resident((d, d)), mem, mem, _resident((d, d))],
        out_specs=tok(d),
        out_shape=jax.ShapeDtypeStruct((bsz, seq, d), F32),
        scratch_shapes=[pltpu.VMEM((tm, d), BF16)],
        compiler_params=_params("parallel", "parallel"),
        name="mixout_xattn",
    )(h, y_ssm, y_attn, w_out.astype(BF16), gain.reshape(1, d), wq.astype(BF16), k_mem, v_mem,
      wo.astype(BF16))


def kernel(x, mem, rel_bias, ffn1_norm, ffn1_w_gate, ffn1_w_up, ffn1_w_down, mix_norm, w_in,
           ssm_lambda_re, ssm_lambda_im, ssm_b_re, ssm_b_im, ssm_c_re, ssm_c_im, ssm_d,
           ssm_log_dt, ssm_w_glu, ssm_b_glu, ssm_out_norm, diff_lambda_q1, diff_lambda_k1,
           diff_lambda_q2, diff_lambda_k2, diff_subln, w_out, xattn_norm, mem_norm, xattn_wq,
           xattn_wkv, xattn_wo, ffn2_norm, ffn2_w_gate, ffn2_w_up, ffn2_w_down, final_norm):
    bsz, seq, d = x.shape
    depth = w_in.shape[0]
    width = w_in.shape[2] // 4
    dqk = width // DIFF_HEADS // 2
    t = bsz * seq
    assert t % TOKEN_TILE == 0 and seq % min(S5_TILE, seq) == 0 and seq % min(ATTN_TILE, seq) == 0
    h = x
    bias_tiles = _rel_bias_tiles(rel_bias, min(ATTN_TILE, seq))
    for l in range(depth):
        lam_init = 0.8 - 0.6 * math.exp(-0.3 * l)
        h = _ffn(h.reshape(t, d), ffn1_norm[l], ffn1_w_gate[l], ffn1_w_up[l], ffn1_w_down[l])
        u, q, k, v = _inproj(h, mix_norm[l], w_in[l], width, dqk ** -0.5 * LOG2E)
        shp = (bsz, seq, width)
        y_ssm = _s5(u.reshape(shp), ssm_lambda_re[l], ssm_lambda_im[l], ssm_b_re[l], ssm_b_im[l],
                    ssm_c_re[l], ssm_c_im[l], ssm_d[l], ssm_log_dt[l], ssm_w_glu[l], ssm_b_glu[l],
                    ssm_out_norm[l])
        lam_params = jnp.stack([diff_lambda_q1[l], diff_lambda_k1[l], diff_lambda_q2[l], diff_lambda_k2[l]])
        y_attn = _diff_attention(q.reshape(shp), k.reshape(shp), v.reshape(shp), bias_tiles,
                                 lam_params.astype(F32), diff_subln[l], lam_init)
        k_mem, v_mem = _memkv(mem, mem_norm[l], xattn_wkv[l])
        h = _mixout_xattn(h.reshape(bsz, seq, d), y_ssm, y_attn, w_out[l], xattn_norm[l], xattn_wq[l],
                          k_mem, v_mem, xattn_wo[l])
        h = _ffn(h.reshape(t, d), ffn2_norm[l], ffn2_w_gate[l], ffn2_w_up[l], ffn2_w_down[l],
                 final_gain=final_norm if l == depth - 1 else None)
    return h.reshape(bsz, seq, d)
```

```python
import functools
import math

import numpy as np
import jax
import jax.numpy as jnp
from jax import lax
from jax.experimental import pallas as pl
from jax.experimental.pallas import tpu as pltpu

F32 = jnp.float32
BF16 = jnp.bfloat16

EPS = 1e-6
SSM_GROUP = 16
DIFF_HEADS = 4
XATTN_HEADS = 4
MAX_DISTANCE = 128
LOG2E = math.log2(math.e)
MASKED = -1e30
SUBLANES = 8
MXU_DIM = 256
VMEM_LIMIT_BYTES = 56 * 1024 * 1024

TOKEN_TILE = 512
FFN_CHUNK = 512
ATTN_TILE = 512
ATTN_ROWS = 128
S5_TILE = 512
S5_COLS = 1024


def _rms(x, gain):
    ms = jnp.mean(x * x, axis=-1, keepdims=True)
    return x * lax.rsqrt(ms + EPS) * gain


def _dot(a, b):
    return jnp.dot(a, b, preferred_element_type=F32)


def _dot_nt(a, b):
    return lax.dot_general(a, b, (((1,), (1,)), ((), ())), preferred_element_type=F32)


def _resident(shape):
    zeros = (0,) * len(shape)
    return pl.BlockSpec(shape, lambda *_: zeros, pipeline_mode=pl.Buffered(1))


def _params(*semantics):
    return pltpu.CompilerParams(dimension_semantics=semantics, vmem_limit_bytes=VMEM_LIMIT_BYTES)


def _ffn_kernel(h_ref, g_ref, wg_ref, wu_ref, wd_ref, *rest, final):
    if final:
        fg_ref, o_ref, a_ref = rest
    else:
        o_ref, a_ref = rest
    n = _rms(h_ref[...], g_ref[...]).astype(BF16)
    d_ff = wg_ref.shape[1]
    for c0 in range(0, d_ff, FFN_CHUNK):
        c1 = min(c0 + FFN_CHUNK, d_ff)
        g = _dot(n, wg_ref[:, c0:c1])
        u = _dot(n, wu_ref[:, c0:c1])
        a_ref[:, c0:c1] = (jax.nn.silu(g) * u).astype(BF16)
    out = h_ref[...] + 0.5 * _dot(a_ref[...], wd_ref[...])
    if final:
        out = _rms(out, fg_ref[...])
    o_ref[...] = out


def _ffn(h, gain, w_gate, w_up, w_down, final_gain=None):
    t, d = h.shape
    d_ff = w_gate.shape[1]
    tm = TOKEN_TILE
    final = final_gain is not None
    row = pl.BlockSpec((tm, d), lambda i: (i, 0))
    in_specs = [row, _resident((1, d)), _resident((d, d_ff)), _resident((d, d_ff)), _resident((d_ff, d))]
    args = [h, gain.reshape(1, d), w_gate.astype(BF16), w_up.astype(BF16), w_down.astype(BF16)]
    if final:
        in_specs.append(_resident((1, d)))
        args.append(final_gain.reshape(1, d))
    return pl.pallas_call(
        functools.partial(_ffn_kernel, final=final),
        grid=(t // tm,),
        in_specs=in_specs,
        out_specs=row,
        out_shape=jax.ShapeDtypeStruct((t, d), F32),
        scratch_shapes=[pltpu.VMEM((tm, d_ff), BF16)],
        compiler_params=_params("parallel"),
        name="ffn_final" if final else "ffn",
    )(*args)


def _inproj_kernel(h_ref, g_ref, w_ref, u_ref, q_ref, k_ref, v_ref, *, q_scale):
    n = _rms(h_ref[...], g_ref[...]).astype(BF16)
    w = u_ref.shape[1]
    u_ref[...] = _dot(n, w_ref[:, 0:w])
    q_ref[...] = (_dot(n, w_ref[:, w:2 * w]) * q_scale).astype(BF16)
    k_ref[...] = _dot(n, w_ref[:, 2 * w:3 * w]).astype(BF16)
    v_ref[...] = _dot(n, w_ref[:, 3 * w:4 * w]).astype(BF16)


def _inproj(h, gain, w_in, width, q_scale):
    t, d = h.shape
    tm = TOKEN_TILE
    assert w_in.shape[1] == 4 * width
    out_spec = pl.BlockSpec((tm, width), lambda i: (i, 0))
    return pl.pallas_call(
        functools.partial(_inproj_kernel, q_scale=q_scale),
        grid=(t // tm,),
        in_specs=[pl.BlockSpec((tm, d), lambda i: (i, 0)), _resident((1, d)), _resident((d, 4 * width))],
        out_specs=[out_spec] * 4,
        out_shape=[jax.ShapeDtypeStruct((t, width), F32)] + [jax.ShapeDtypeStruct((t, width), BF16)] * 3,
        compiler_params=_params("parallel"),
        name="inproj",
    )(h, gain.reshape(1, d), w_in.astype(BF16))


def _s5_kernel(u_ref, perm3_ref, perm_t_ref, bre_ref, bim_ref, lre_ref, lim_ref, ltre_ref, ltim_ref,
               c_ref, d_ref, wglu_ref, bglu_ref, gn_ref, o_ref, x_ref, carry_re_ref, carry_im_ref):
    tl = u_ref.shape[1]
    tc = tl // SUBLANES
    width = u_ref.shape[2]
    n_kb = width // MXU_DIM
    sb = x_ref.shape[1] // (2 * n_kb)

    @pl.when(pl.program_id(1) == 0)
    def _():
        carry_re_ref[...] = jnp.zeros_like(carry_re_ref)
        carry_im_ref[...] = jnp.zeros_like(carry_im_ref)

    u = u_ref[0]
    u_hi = u.astype(BF16)
    rem = u - u_hi.astype(F32)
    u_mid = rem.astype(BF16)
    u_lo = (rem - u_mid.astype(F32)).astype(BF16)
    up = _dot(perm3_ref[...], jnp.concatenate([u_hi, u_mid, u_lo], axis=0))
    up_b = _dot(perm3_ref[:, :tl], u_hi).astype(BF16)

    for kb in range(n_kb):
        ub = up_b[:, kb * MXU_DIM:(kb + 1) * MXU_DIM]
        x_ref[:, 2 * kb * sb:(2 * kb + 1) * sb] = _dot(ub, bre_ref[kb])
        x_ref[:, (2 * kb + 1) * sb:(2 * kb + 2) * sb] = _dot(ub, bim_ref[kb])

    for kb in range(n_kb):
        for c0 in range(0, sb, S5_COLS):
            lam_cols = slice(kb * sb + c0, kb * sb + c0 + S5_COLS)
            re_cols = slice(2 * kb * sb + c0, 2 * kb * sb + c0 + S5_COLS)
            im_cols = slice((2 * kb + 1) * sb + c0, (2 * kb + 1) * sb + c0 + S5_COLS)
            lr = jnp.broadcast_to(lre_ref[:, lam_cols], (SUBLANES, S5_COLS))
            li = jnp.broadcast_to(lim_ref[:, lam_cols], (SUBLANES, S5_COLS))

            def step(j, xr, xi):
                r = pl.multiple_of(j * SUBLANES, SUBLANES)
                br = x_ref[pl.ds(r, SUBLANES), re_cols]
                bi = x_ref[pl.ds(r, SUBLANES), im_cols]
                return r, lr * xr - li * xi + br, lr * xi + li * xr + bi

            def end_state_body(j, x):
                _, nr, ni = step(j, *x)
                return nr, ni

            zero = jnp.zeros((SUBLANES, S5_COLS), F32)
            end_re, end_im = lax.fori_loop(0, tc, end_state_body, (zero, zero), unroll=4)

            ltr = ltre_ref[:, lam_cols]
            lti = ltim_ref[:, lam_cols]
            cr = carry_re_ref[:, lam_cols]
            ci = carry_im_ref[:, lam_cols]
            init_re, init_im = [], []
            for k in range(SUBLANES):
                init_re.append(cr)
                init_im.append(ci)
                cr, ci = (ltr * cr - lti * ci + end_re[k:k + 1], ltr * ci + lti * cr + end_im[k:k + 1])
            carry_re_ref[:, lam_cols] = cr
            carry_im_ref[:, lam_cols] = ci

            def store_body(j, x):
                r, nr, ni = step(j, *x)
                x_ref[pl.ds(r, SUBLANES), re_cols] = nr
                x_ref[pl.ds(r, SUBLANES), im_cols] = ni
                return nr, ni

            lax.fori_loop(0, tc, store_body,
                          (jnp.concatenate(init_re, axis=0), jnp.concatenate(init_im, axis=0)), unroll=4)

    parts = [_dot(x_ref[:, 2 * kb * sb:(2 * kb + 2) * sb].astype(BF16), c_ref[kb]) for kb in range(n_kb)]
    y = jnp.concatenate(parts, axis=1) + d_ref[...] * up
    z = _dot(jax.nn.gelu(y).astype(BF16), wglu_ref[...]) + bglu_ref[...]
    gated = z[:, :width] * jax.nn.sigmoid(z[:, width:])
    out = _rms(gated, gn_ref[...]).astype(BF16)
    o_ref[0] = _dot(perm_t_ref[...], out).astype(BF16)


def _s5_discretise(lam_re, lam_im, b_re, b_im, log_dt, n_sub):
    a = jnp.minimum(lam_re, -1e-4)
    dt = jnp.exp(log_dt)[:, None]
    za, zb = a * dt, lam_im * dt
    ea = jnp.exp(za)
    bar_re, bar_im = ea * jnp.cos(zb), ea * jnp.sin(zb)
    num_re = jnp.expm1(za) * jnp.cos(zb) - 2.0 * jnp.sin(0.5 * zb) ** 2
    num_im = bar_im
    den = a * a + lam_im * lam_im
    f_re = (num_re * a + num_im * lam_im) / den
    f_im = (num_im * a - num_re * lam_im) / den
    bb_re = f_re[:, :, None] * b_re - f_im[:, :, None] * b_im
    bb_im = f_re[:, :, None] * b_im + f_im[:, :, None] * b_re
    en = jnp.exp(za * n_sub)
    pow_re, pow_im = en * jnp.cos(zb * n_sub), en * jnp.sin(zb * n_sub)
    return bar_re, bar_im, bb_re, bb_im, pow_re, pow_im


def _s5(u, lam_re, lam_im, b_re, b_im, c_re, c_im, d, log_dt, w_glu, b_glu, out_norm):
    bsz, seq, width = u.shape
    n_groups, n_p = lam_re.shape
    gh = SSM_GROUP
    assert n_groups * gh == width and width % MXU_DIM == 0
    tl = min(S5_TILE, seq)
    tc = tl // SUBLANES
    n_state = n_groups * n_p
    n_kb = width // MXU_DIM
    gpb = n_groups // n_kb
    assert (gpb * n_p) % S5_COLS == 0
    bar_re, bar_im, bb_re, bb_im, pow_re, pow_im = _s5_discretise(lam_re, lam_im, b_re, b_im, log_dt, tc)

    eye = jnp.eye(gpb, dtype=F32)

    def in_blocks(bb):
        bb = bb.reshape(n_kb, gpb, n_p, gh)
        return jnp.einsum('kgph,gj->kghjp', bb, eye).reshape(n_kb, gpb * gh, gpb * n_p).astype(BF16)

    def out_blocks(c):
        c = c.reshape(n_kb, gpb, gh, n_p)
        return jnp.einsum('kghp,gj->kgpjh', c, eye).reshape(n_kb, gpb * n_p, gpb * gh).astype(BF16)

    rows = np.arange(tl)
    perm_np = np.zeros((tl, tl), np.float32)
    perm_np[rows, (rows % SUBLANES) * tc + rows // SUBLANES] = 1.0
    perm3 = jnp.asarray(np.concatenate([perm_np] * 3, axis=1), BF16)
    perm_t = jnp.asarray(perm_np.T, BF16)

    flat = lambda x: x.reshape(1, n_state)
    c_blocks = jnp.concatenate([out_blocks(c_re), out_blocks(-c_im)], axis=1)
    args = [u, perm3, perm_t, in_blocks(bb_re), in_blocks(bb_im), flat(bar_re), flat(bar_im),
            flat(pow_re), flat(pow_im), c_blocks, d.reshape(1, width),
            w_glu.astype(BF16), b_glu.reshape(1, 2 * width), out_norm.reshape(1, width)]
    tile = pl.BlockSpec((1, tl, width), lambda b, l: (b, l, 0))
    in_specs = [tile] + [_resident(a.shape) for a in args[1:]]
    return pl.pallas_call(
        _s5_kernel,
        grid=(bsz, seq // tl),
        in_specs=in_specs,
        out_specs=tile,
        out_shape=jax.ShapeDtypeStruct((bsz, seq, width), BF16),
        scratch_shapes=[pltpu.VMEM((tl, 2 * n_state), F32),
                        pltpu.VMEM((1, n_state), F32), pltpu.VMEM((1, n_state), F32)],
        compiler_params=_params("parallel", "arbitrary"),
        name="s5",
    )(*args)


def _rel_bias_tiles(rel_bias, tile):
    n_buckets = rel_bias.shape[0]
    max_exact = n_buckets // 2

    def bucket(rel):
        n = np.maximum(rel, 0)
        n_f = np.maximum(n, 1).astype(np.float32)
        large = max_exact + (np.log(n_f / np.float32(max_exact)) / np.float32(math.log(MAX_DISTANCE / max_exact))
                             * np.float32(n_buckets - max_exact)).astype(np.int32)
        return np.where(n < max_exact, n, np.minimum(large, n_buckets - 1))

    assert bucket(np.array([tile + 1]))[0] == n_buckets - 1, "far key tiles must share one bucket"
    rel0 = np.arange(tile)[:, None] - np.arange(tile)[None, :]
    table = rel_bias.astype(F32)
    shifted = (table - table[n_buckets - 1][None, :]).T * LOG2E

    def lookup(rel):
        onehot = (jnp.asarray(bucket(rel).reshape(-1))[None, :] == jnp.arange(n_buckets)[:, None]).astype(F32)
        return jnp.dot(shifted, onehot, precision=lax.Precision.HIGHEST).reshape(-1, tile, tile)

    diag = jnp.where(jnp.asarray(rel0 >= 0)[None], lookup(rel0), MASKED)
    sub = lookup(rel0 + tile)
    tiles = jnp.stack([diag, sub, jnp.zeros_like(sub), jnp.full_like(sub, MASKED)], axis=1)
    return tiles.transpose(0, 1, 3, 2)


def _attn_kernel(q_ref, k_ref, v_ref, bias_ref, lp_ref, sub_ref, o_ref, sa_ref, sb_ref, *, lam_init):
    tq = q_ref.shape[1]
    dv = q_ref.shape[2]
    dqk = dv // 2
    i = pl.program_id(2)
    q = q_ref[0]
    q1, q2 = q[:, :dqk], q[:, dqk:]

    def update(s_ref, mp, state, vt):
        m, l, acc = state
        m_new = jnp.maximum(m, jnp.max(s_ref[mp], axis=0, keepdims=True))
        alpha = jnp.exp2(m - m_new)
        half = tq // 2
        p = jnp.concatenate([jnp.exp2(s_ref[mp, :half, :] - m_new), jnp.exp2(s_ref[mp, half:, :] - m_new)],
                            axis=0)
        l = alpha * l + jnp.sum(p, axis=0, keepdims=True)
        pv = lax.dot_general(vt, p.astype(BF16), (((0,), (0,)), ((), ())), preferred_element_type=F32)
        return m_new, l, alpha * acc + pv

    def issue_scores(t, s_ref):
        kt = k_ref[0, pl.ds(pl.multiple_of(jnp.minimum(t, i) * tq, tq), tq), :]
        bias = bias_ref[0, jnp.where(t > i, 3, jnp.minimum(i - t, 2))]
        s_ref[0] = _dot_nt(kt[:, :dqk], q1) + bias
        s_ref[1] = _dot_nt(kt[:, dqk:], q2) + bias

    def consume(t, s_ref, states):
        vt = v_ref[0, pl.ds(pl.multiple_of(jnp.minimum(t, i) * tq, tq), tq), :]
        return update(s_ref, 0, states[0], vt), update(s_ref, 1, states[1], vt)

    def pair_body(p, states):
        issue_scores(2 * p + 1, sb_ref)
        states = consume(2 * p, sa_ref, states)
        issue_scores(2 * p + 2, sa_ref)
        return consume(2 * p + 1, sb_ref, states)

    init = (jnp.full((1, tq), -jnp.inf, F32), jnp.zeros((1, tq), F32), jnp.zeros((dv, tq), F32))
    issue_scores(0, sa_ref)
    (_, l1, a1), (_, l2, a2) = lax.fori_loop(0, (i + 2) // 2, pair_body, (init, init))

    lp = lp_ref[...]
    lam = (jnp.exp(jnp.sum(lp[0:1] * lp[1:2], axis=-1, keepdims=True))
           - jnp.exp(jnp.sum(lp[2:3] * lp[3:4], axis=-1, keepdims=True)) + lam_init)
    o = a1 / l1 - lam * (a2 / l2)
    o = o * lax.rsqrt(jnp.mean(o * o, axis=0, keepdims=True) + EPS) * sub_ref[...]
    o_ref[0] = (o * (1.0 - lam_init)).T.astype(BF16)


def _diff_attention(q, k, v, bias, lam_params, subln, lam_init):
    bsz, seq, width = q.shape
    dv = width // DIFF_HEADS
    tq = bias.shape[2]
    q_spec = pl.BlockSpec((1, tq, dv), lambda b, h, i: (b, i, h))
    kv_spec = pl.BlockSpec((1, seq, dv), lambda b, h, i: (b, 0, h))
    return pl.pallas_call(
        functools.partial(_attn_kernel, lam_init=lam_init),
        grid=(bsz, DIFF_HEADS, seq // tq),
        in_specs=[q_spec, kv_spec, kv_spec,
                  pl.BlockSpec((1, 4, tq, tq), lambda b, h, i: (h, 0, 0, 0)),
                  _resident(lam_params.shape), _resident((dv, 1))],
        out_specs=q_spec,
        out_shape=jax.ShapeDtypeStruct((bsz, seq, width), BF16),
        scratch_shapes=[pltpu.VMEM((2, tq, tq), F32)] * 2,
        compiler_params=_params("parallel", "parallel", "parallel"),
        name="diff_attn",
    )(q, k, v, bias, lam_params, subln.reshape(dv, 1))


def _memkv_kernel(m_ref, g_ref, w_ref, k_ref, v_ref):
    d = m_ref.shape[2]
    n = _rms(m_ref[0], g_ref[...]).astype(BF16)
    k_ref[0] = _dot(n, w_ref[:, :d]).astype(BF16)
    v_ref[0] = _dot(n, w_ref[:, d:]).astype(BF16)


def _memkv(mem, gain, wkv):
    bsz, m_len, d = mem.shape
    blk = pl.BlockSpec((1, m_len, d), lambda b: (b, 0, 0))
    return pl.pallas_call(
        _memkv_kernel,
        grid=(bsz,),
        in_specs=[blk, _resident((1, d)), _resident((d, 2 * d))],
        out_specs=[blk, blk],
        out_shape=[jax.ShapeDtypeStruct((bsz, m_len, d), BF16)] * 2,
        compiler_params=_params("parallel"),
        name="memkv",
    )(mem, gain.reshape(1, d), wkv.astype(BF16))


def _mixout_xattn_kernel(h_ref, ys_ref, ya_ref, wout_ref, g_ref, wq_ref, km_ref, vm_ref, wo_ref,
                         o_ref, oh_ref, *, q_scale):
    h2 = h_ref[0] + _dot(jnp.concatenate([ys_ref[0], ya_ref[0]], axis=1), wout_ref[...])
    n = _rms(h2, g_ref[...]).astype(BF16)
    q = (_dot(n, wq_ref[...]) * q_scale).astype(BF16)
    d = q.shape[1]
    hd = d // XATTN_HEADS
    for hh in range(XATTN_HEADS):
        sl = slice(hh * hd, (hh + 1) * hd)
        s = _dot_nt(q[:, sl], km_ref[0, :, sl])
        p = jnp.exp(s - jnp.max(s, axis=-1, keepdims=True))
        o = _dot(p.astype(BF16), vm_ref[0, :, sl]) / jnp.sum(p, axis=-1, keepdims=True)
        oh_ref[:, sl] = o.astype(BF16)
    o_ref[0] = h2 + _dot(oh_ref[...], wo_ref[...])


def _mixout_xattn(h, y_ssm, y_attn, w_out, gain, wq, k_mem, v_mem, wo):
    bsz, seq, d = h.shape
    ws, wa = y_ssm.shape[2], y_attn.shape[2]
    m_len = k_mem.shape[1]
    tm = min(TOKEN_TILE, seq)
    tok = lambda w: pl.BlockSpec((1, tm, w), lambda b, i: (b, i, 0))
    mem = pl.BlockSpec((1, m_len, d), lambda b, i: (b, 0, 0))
    return pl.pallas_call(
        functools.partial(_mixout_xattn_kernel, q_scale=(d // XATTN_HEADS) ** -0.5),
        grid=(bsz, seq // tm),
        in_specs=[tok(d), tok(ws), tok(wa), _resident((ws + wa, d)), _resident((1, d)),
                  _resident((d, d)), mem, mem, _resident((d, d))],
        out_specs=tok(d),
        out_shape=jax.ShapeDtypeStruct((bsz, seq, d), F32),
        scratch_shapes=[pltpu.VMEM((tm, d), BF16)],
        compiler_params=_params("parallel", "parallel"),
        name="mixout_xattn",
    )(h, y_ssm, y_attn, w_out.astype(BF16), gain.reshape(1, d), wq.astype(BF16), k_mem, v_mem,
      wo.astype(BF16))


def kernel(x, mem, rel_bias, ffn1_norm, ffn1_w_gate, ffn1_w_up, ffn1_w_down, mix_norm, w_in,
           ssm_lambda_re, ssm_lambda_im, ssm_b_re, ssm_b_im, ssm_c_re, ssm_c_im, ssm_d,
           ssm_log_dt, ssm_w_glu, ssm_b_glu, ssm_out_norm, diff_lambda_q1, diff_lambda_k1,
           diff_lambda_q2, diff_lambda_k2, diff_subln, w_out, xattn_norm, mem_norm, xattn_wq,
           xattn_wkv, xattn_wo, ffn2_norm, ffn2_w_gate, ffn2_w_up, ffn2_w_down, final_norm):
    bsz, seq, d = x.shape
    depth = w_in.shape[0]
    width = w_in.shape[2] // 4
    dqk = width // DIFF_HEADS // 2
    t = bsz * seq
    assert t % TOKEN_TILE == 0 and seq % min(S5_TILE, seq) == 0 and seq % min(ATTN_TILE, seq) == 0
    h = x
    bias_tiles = _rel_bias_tiles(rel_bias, min(ATTN_TILE, seq))
    for l in range(depth):
        lam_init = 0.8 - 0.6 * math.exp(-0.3 * l)
        h = _ffn(h.reshape(t, d), ffn1_norm[l], ffn1_w_gate[l], ffn1_w_up[l], ffn1_w_down[l])
        u, q, k, v = _inproj(h, mix_norm[l], w_in[l], width, dqk ** -0.5 * LOG2E)
        shp = (bsz, seq, width)
        y_ssm = _s5(u.reshape(shp), ssm_lambda_re[l], ssm_lambda_im[l], ssm_b_re[l], ssm_b_im[l],
                    ssm_c_re[l], ssm_c_im[l], ssm_d[l], ssm_log_dt[l], ssm_w_glu[l], ssm_b_glu[l],
                    ssm_out_norm[l])
        lam_params = jnp.stack([diff_lambda_q1[l], diff_lambda_k1[l], diff_lambda_q2[l], diff_lambda_k2[l]])
        y_attn = _diff_attention(q.reshape(shp), k.reshape(shp), v.reshape(shp), bias_tiles,
                                 lam_params.astype(F32), diff_subln[l], lam_init)
        k_mem, v_mem = _memkv(mem, mem_norm[l], xattn_wkv[l])
        h = _mixout_xattn(h.reshape(bsz, seq, d), y_ssm, y_attn, w_out[l], xattn_norm[l], xattn_wq[l],
                          k_mem, v_mem, xattn_wo[l])
        h = _ffn(h.reshape(t, d), ffn2_norm[l], ffn2_w_gate[l], ffn2_w_up[l], ffn2_w_down[l],
                 final_gain=final_norm if l == depth - 1 else None)
    return h.reshape(bsz, seq, d)
```

```python
import functools
import math

import numpy as np
import jax
import jax.numpy as jnp
from jax import lax
from jax.experimental import pallas as pl
from jax.experimental.pallas import tpu as pltpu

F32 = jnp.float32
BF16 = jnp.bfloat16

EPS = 1e-6
SSM_GROUP = 16
DIFF_HEADS = 4
XATTN_HEADS = 4
MAX_DISTANCE = 128
LOG2E = math.log2(math.e)
MASKED = -1e30
SUBLANES = 8
MXU_DIM = 256
VMEM_LIMIT_BYTES = 56 * 1024 * 1024

TOKEN_TILE = 512
FFN_CHUNK = 512
ATTN_TILE = 512
ATTN_ROWS = 128
S5_TILE = 512
S5_COLS = 1024


def _rms(x, gain):
    ms = jnp.mean(x * x, axis=-1, keepdims=True)
    return x * lax.rsqrt(ms + EPS) * gain


def _dot(a, b):
    return jnp.dot(a, b, preferred_element_type=F32)


def _dot_nt(a, b):
    return lax.dot_general(a, b, (((1,), (1,)), ((), ())), preferred_element_type=F32)


def _resident(shape):
    zeros = (0,) * len(shape)
    return pl.BlockSpec(shape, lambda *_: zeros, pipeline_mode=pl.Buffered(1))


def _params(*semantics):
    return pltpu.CompilerParams(dimension_semantics=semantics, vmem_limit_bytes=VMEM_LIMIT_BYTES)


def _ffn_kernel(h_ref, g_ref, wg_ref, wu_ref, wd_ref, *rest, final):
    if final:
        fg_ref, o_ref, a_ref = rest
    else:
        o_ref, a_ref = rest
    n = _rms(h_ref[...], g_ref[...]).astype(BF16)
    d_ff = wg_ref.shape[1]
    for c0 in range(0, d_ff, FFN_CHUNK):
        c1 = min(c0 + FFN_CHUNK, d_ff)
        g = _dot(n, wg_ref[:, c0:c1])
        u = _dot(n, wu_ref[:, c0:c1])
        a_ref[:, c0:c1] = (jax.nn.silu(g) * u).astype(BF16)
    out = h_ref[...] + 0.5 * _dot(a_ref[...], wd_ref[...])
    if final:
        out = _rms(out, fg_ref[...])
    o_ref[...] = out


def _ffn(h, gain, w_gate, w_up, w_down, final_gain=None):
    t, d = h.shape
    d_ff = w_gate.shape[1]
    tm = TOKEN_TILE
    final = final_gain is not None
    row = pl.BlockSpec((tm, d), lambda i: (i, 0))
    in_specs = [row, _resident((1, d)), _resident((d, d_ff)), _resident((d, d_ff)), _resident((d_ff, d))]
    args = [h, gain.reshape(1, d), w_gate.astype(BF16), w_up.astype(BF16), w_down.astype(BF16)]
    if final:
        in_specs.append(_resident((1, d)))
        args.append(final_gain.reshape(1, d))
    return pl.pallas_call(
        functools.partial(_ffn_kernel, final=final),
        grid=(t // tm,),
        in_specs=in_specs,
        out_specs=row,
        out_shape=jax.ShapeDtypeStruct((t, d), F32),
        scratch_shapes=[pltpu.VMEM((tm, d_ff), BF16)],
        compiler_params=_params("parallel"),
        name="ffn_final" if final else "ffn",
    )(*args)


def _inproj_kernel(h_ref, g_ref, w_ref, u_ref, q_ref, k_ref, v_ref, *, q_scale):
    n = _rms(h_ref[...], g_ref[...]).astype(BF16)
    w = u_ref.shape[1]
    u_ref[...] = _dot(n, w_ref[:, 0:w])
    q_ref[...] = (_dot(n, w_ref[:, w:2 * w]) * q_scale).astype(BF16)
    k_ref[...] = _dot(n, w_ref[:, 2 * w:3 * w]).astype(BF16)
    v_ref[...] = _dot(n, w_ref[:, 3 * w:4 * w]).astype(BF16)


def _inproj(h, gain, w_in, width, q_scale):
    t, d = h.shape
    tm = TOKEN_TILE
    assert w_in.shape[1] == 4 * width
    out_spec = pl.BlockSpec((tm, width), lambda i: (i, 0))
    return pl.pallas_call(
        functools.partial(_inproj_kernel, q_scale=q_scale),
        grid=(t // tm,),
        in_specs=[pl.BlockSpec((tm, d), lambda i: (i, 0)), _resident((1, d)), _resident((d, 4 * width))],
        out_specs=[out_spec] * 4,
        out_shape=[jax.ShapeDtypeStruct((t, width), F32)] + [jax.ShapeDtypeStruct((t, width), BF16)] * 3,
        compiler_params=_params("parallel"),
        name="inproj",
    )(h, gain.reshape(1, d), w_in.astype(BF16))


def _s5_kernel(u_ref, perm3_ref, perm_t_ref, bre_ref, bim_ref, lre_ref, lim_ref, ltre_ref, ltim_ref,
               c_ref, d_ref, wglu_ref, bglu_ref, gn_ref, o_ref, x_ref, carry_re_ref, carry_im_ref):
    tl = u_ref.shape[1]
    tc = tl // SUBLANES
    width = u_ref.shape[2]
    n_kb = width // MXU_DIM
    sb = x_ref.shape[1] // (2 * n_kb)

    @pl.when(pl.program_id(1) == 0)
    def _():
        carry_re_ref[...] = jnp.zeros_like(carry_re_ref)
        carry_im_ref[...] = jnp.zeros_like(carry_im_ref)

    u = u_ref[0]
    u_hi = u.astype(BF16)
    rem = u - u_hi.astype(F32)
    u_mid = rem.astype(BF16)
    u_lo = (rem - u_mid.astype(F32)).astype(BF16)
    up = _dot(perm3_ref[...], jnp.concatenate([u_hi, u_mid, u_lo], axis=0))
    up_b = _dot(perm3_ref[:, :tl], u_hi).astype(BF16)

    for kb in range(n_kb):
        ub = up_b[:, kb * MXU_DIM:(kb + 1) * MXU_DIM]
        x_ref[:, 2 * kb * sb:(2 * kb + 1) * sb] = _dot(ub, bre_ref[kb])
        x_ref[:, (2 * kb + 1) * sb:(2 * kb + 2) * sb] = _dot(ub, bim_ref[kb])

    for kb in range(n_kb):
        for c0 in range(0, sb, S5_COLS):
            lam_cols = slice(kb * sb + c0, kb * sb + c0 + S5_COLS)
            re_cols = slice(2 * kb * sb + c0, 2 * kb * sb + c0 + S5_COLS)
            im_cols = slice((2 * kb + 1) * sb + c0, (2 * kb + 1) * sb + c0 + S5_COLS)
            lr = jnp.broadcast_to(lre_ref[:, lam_cols], (SUBLANES, S5_COLS))
            li = jnp.broadcast_to(lim_ref[:, lam_cols], (SUBLANES, S5_COLS))

            def step(j, xr, xi):
                r = pl.multiple_of(j * SUBLANES, SUBLANES)
                br = x_ref[pl.ds(r, SUBLANES), re_cols]
                bi = x_ref[pl.ds(r, SUBLANES), im_cols]
                return r, lr * xr - li * xi + br, lr * xi + li * xr + bi

            def end_state_body(j, x):
                _, nr, ni = step(j, *x)
                return nr, ni

            zero = jnp.zeros((SUBLANES, S5_COLS), F32)
            end_re, end_im = lax.fori_loop(0, tc, end_state_body, (zero, zero), unroll=4)

            ltr = ltre_ref[:, lam_cols]
            lti = ltim_ref[:, lam_cols]
            cr = carry_re_ref[:, lam_cols]
            ci = carry_im_ref[:, lam_cols]
            init_re, init_im = [], []
            for k in range(SUBLANES):
                init_re.append(cr)
                init_im.append(ci)
                cr, ci = (ltr * cr - lti * ci + end_re[k:k + 1], ltr * ci + lti * cr + end_im[k:k + 1])
            carry_re_ref[:, lam_cols] = cr
            carry_im_ref[:, lam_cols] = ci

            def store_body(j, x):
                r, nr, ni = step(j, *x)
                x_ref[pl.ds(r, SUBLANES), re_cols] = nr
                x_ref[pl.ds(r, SUBLANES), im_cols] = ni
                return nr, ni

            lax.fori_loop(0, tc, store_body,
                          (jnp.concatenate(init_re, axis=0), jnp.concatenate(init_im, axis=0)), unroll=4)

    parts = [_dot(x_ref[:, 2 * kb * sb:(2 * kb + 2) * sb].astype(BF16), c_ref[kb]) for kb in range(n_kb)]
    y = jnp.concatenate(parts, axis=1) + d_ref[...] * up
    z = _dot(jax.nn.gelu(y).astype(BF16), wglu_ref[...]) + bglu_ref[...]
    gated = z[:, :width] * jax.nn.sigmoid(z[:, width:])
    out = _rms(gated, gn_ref[...]).astype(BF16)
    o_ref[0] = _dot(perm_t_ref[...], out).astype(BF16)


def _s5_discretise(lam_re, lam_im, b_re, b_im, log_dt, n_sub):
    a = jnp.minimum(lam_re, -1e-4)
    dt = jnp.exp(log_dt)[:, None]
    za, zb = a * dt, lam_im * dt
    ea = jnp.exp(za)
    bar_re, bar_im = ea * jnp.cos(zb), ea * jnp.sin(zb)
    num_re = jnp.expm1(za) * jnp.cos(zb) - 2.0 * jnp.sin(0.5 * zb) ** 2
    num_im = bar_im
    den = a * a + lam_im * lam_im
    f_re = (num_re * a + num_im * lam_im) / den
    f_im = (num_im * a - num_re * lam_im) / den
    bb_re = f_re[:, :, None] * b_re - f_im[:, :, None] * b_im
    bb_im = f_re[:, :, None] * b_im + f_im[:, :, None] * b_re
    en = jnp.exp(za * n_sub)
    pow_re, pow_im = en * jnp.cos(zb * n_sub), en * jnp.sin(zb * n_sub)
    return bar_re, bar_im, bb_re, bb_im, pow_re, pow_im


def _s5(u, lam_re, lam_im, b_re, b_im, c_re, c_im, d, log_dt, w_glu, b_glu, out_norm):
    bsz, seq, width = u.shape
    n_groups, n_p = lam_re.shape
    gh = SSM_GROUP
    assert n_groups * gh == width and width % MXU_DIM == 0
    tl = min(S5_TILE, seq)
    tc = tl // SUBLANES
    n_state = n_groups * n_p
    n_kb = width // MXU_DIM
    gpb = n_groups // n_kb
    assert (gpb * n_p) % S5_COLS == 0
    bar_re, bar_im, bb_re, bb_im, pow_re, pow_im = _s5_discretise(lam_re, lam_im, b_re, b_im, log_dt, tc)

    eye = jnp.eye(gpb, dtype=F32)

    def in_blocks(bb):
        bb = bb.reshape(n_kb, gpb, n_p, gh)
        return jnp.einsum('kgph,gj->kghjp', bb, eye).reshape(n_kb, gpb * gh, gpb * n_p).astype(BF16)

    def out_blocks(c):
        c = c.reshape(n_kb, gpb, gh, n_p)
        return jnp.einsum('kghp,gj->kgpjh', c, eye).reshape(n_kb, gpb * n_p, gpb * gh).astype(BF16)

    rows = np.arange(tl)
    perm_np = np.zeros((tl, tl), np.float32)
    perm_np[rows, (rows % SUBLANES) * tc + rows // SUBLANES] = 1.0
    perm3 = jnp.asarray(np.concatenate([perm_np] * 3, axis=1), BF16)
    perm_t = jnp.asarray(perm_np.T, BF16)

    flat = lambda x: x.reshape(1, n_state)
    c_blocks = jnp.concatenate([out_blocks(c_re), out_blocks(-c_im)], axis=1)
    args = [u, perm3, perm_t, in_blocks(bb_re), in_blocks(bb_im), flat(bar_re), flat(bar_im),
            flat(pow_re), flat(pow_im), c_blocks, d.reshape(1, width),
            w_glu.astype(BF16), b_glu.reshape(1, 2 * width), out_norm.reshape(1, width)]
    tile = pl.BlockSpec((1, tl, width), lambda b, l: (b, l, 0))
    in_specs = [tile] + [_resident(a.shape) for a in args[1:]]
    return pl.pallas_call(
        _s5_kernel,
        grid=(bsz, seq // tl),
        in_specs=in_specs,
        out_specs=tile,
        out_shape=jax.ShapeDtypeStruct((bsz, seq, width), BF16),
        scratch_shapes=[pltpu.VMEM((tl, 2 * n_state), F32),
                        pltpu.VMEM((1, n_state), F32), pltpu.VMEM((1, n_state), F32)],
        compiler_params=_params("parallel", "arbitrary"),
        name="s5",
    )(*args)


def _rel_bias_tiles(rel_bias, tile):
    n_buckets = rel_bias.shape[0]
    max_exact = n_buckets // 2

    def bucket(rel):
        n = np.maximum(rel, 0)
        n_f = np.maximum(n, 1).astype(np.float32)
        large = max_exact + (np.log(n_f / np.float32(max_exact)) / np.float32(math.log(MAX_DISTANCE / max_exact))
                             * np.float32(n_buckets - max_exact)).astype(np.int32)
        return np.where(n < max_exact, n, np.minimum(large, n_buckets - 1))

    assert bucket(np.array([tile + 1]))[0] == n_buckets - 1, "far key tiles must share one bucket"
    rel0 = np.arange(tile)[:, None] - np.arange(tile)[None, :]
    table = rel_bias.astype(F32)
    shifted = (table - table[n_buckets - 1][None, :]).T * LOG2E

    def lookup(rel):
        onehot = (jnp.asarray(bucket(rel).reshape(-1))[None, :] == jnp.arange(n_buckets)[:, None]).astype(F32)
        return jnp.dot(shifted, onehot, precision=lax.Precision.HIGHEST).reshape(-1, tile, tile)

    diag = jnp.where(jnp.asarray(rel0 >= 0)[None], lookup(rel0), MASKED)
    sub = lookup(rel0 + tile)
    tiles = jnp.stack([diag, sub, jnp.zeros_like(sub)], axis=1)
    return tiles.transpose(0, 1, 3, 2)


def _attn_kernel(q_ref, k_ref, v_ref, bias_ref, lp_ref, sub_ref, o_ref, sa_ref, sb_ref, *, lam_init):
    tq = q_ref.shape[1]
    dv = q_ref.shape[2]
    dqk = dv // 2
    i = pl.program_id(2)
    q = q_ref[0]
    q1, q2 = q[:, :dqk], q[:, dqk:]

    def update(s_ref, mp, state, vt):
        m, l, acc = state
        m_new = jnp.maximum(m, jnp.max(s_ref[mp], axis=0, keepdims=True))
        alpha = jnp.exp2(m - m_new)
        half = tq // 2
        p = jnp.concatenate([jnp.exp2(s_ref[mp, :half, :] - m_new), jnp.exp2(s_ref[mp, half:, :] - m_new)],
                            axis=0)
        l = alpha * l + jnp.sum(p, axis=0, keepdims=True)
        pv = lax.dot_general(vt, p.astype(BF16), (((0,), (0,)), ((), ())), preferred_element_type=F32)
        return m_new, l, alpha * acc + pv

    def issue_scores(t, s_ref):
        t = jnp.minimum(t, i)
        kt = k_ref[0, pl.ds(pl.multiple_of(t * tq, tq), tq), :]
        bias = bias_ref[0, jnp.minimum(i - t, 2)]
        s_ref[0] = _dot_nt(kt[:, :dqk], q1) + bias
        s_ref[1] = _dot_nt(kt[:, dqk:], q2) + bias

    def consume(t, s_ref, states):
        vt = v_ref[0, pl.ds(pl.multiple_of(t * tq, tq), tq), :]
        return update(s_ref, 0, states[0], vt), update(s_ref, 1, states[1], vt)

    def pair_body(p, states):
        issue_scores(2 * p + 1, sb_ref)
        states = consume(2 * p, sa_ref, states)
        issue_scores(2 * p + 2, sa_ref)
        return consume(2 * p + 1, sb_ref, states)

    init = (jnp.full((1, tq), -jnp.inf, F32), jnp.zeros((1, tq), F32), jnp.zeros((dv, tq), F32))
    issue_scores(0, sa_ref)
    n_tiles = i + 1
    states = lax.fori_loop(0, n_tiles // 2, pair_body, (init, init))
    (_, l1, a1), (_, l2, a2) = lax.cond(n_tiles % 2 == 1, lambda st: consume(i, sa_ref, st),
                                        lambda st: st, states)

    lp = lp_ref[...]
    lam = (jnp.exp(jnp.sum(lp[0:1] * lp[1:2], axis=-1, keepdims=True))
           - jnp.exp(jnp.sum(lp[2:3] * lp[3:4], axis=-1, keepdims=True)) + lam_init)
    o = a1 / l1 - lam * (a2 / l2)
    o = o * lax.rsqrt(jnp.mean(o * o, axis=0, keepdims=True) + EPS) * sub_ref[...]
    o_ref[0] = (o * (1.0 - lam_init)).T.astype(BF16)


def _diff_attention(q, k, v, bias, lam_params, subln, lam_init):
    bsz, seq, width = q.shape
    dv = width // DIFF_HEADS
    tq = bias.shape[2]
    q_spec = pl.BlockSpec((1, tq, dv), lambda b, h, i: (b, i, h))
    kv_spec = pl.BlockSpec((1, seq, dv), lambda b, h, i: (b, 0, h))
    return pl.pallas_call(
        functools.partial(_attn_kernel, lam_init=lam_init),
        grid=(bsz, DIFF_HEADS, seq // tq),
        in_specs=[q_spec, kv_spec, kv_spec,
                  pl.BlockSpec((1, 3, tq, tq), lambda b, h, i: (h, 0, 0, 0)),
                  _resident(lam_params.shape), _resident((dv, 1))],
        out_specs=q_spec,
        out_shape=jax.ShapeDtypeStruct((bsz, seq, width), BF16),
        scratch_shapes=[pltpu.VMEM((2, tq, tq), F32)] * 2,
        compiler_params=_params("parallel", "parallel", "parallel"),
        name="diff_attn",
    )(q, k, v, bias, lam_params, subln.reshape(dv, 1))


def _memkv_kernel(m_ref, g_ref, w_ref, k_ref, v_ref):
    d = m_ref.shape[2]
    n = _rms(m_ref[0], g_ref[...]).astype(BF16)
    k_ref[0] = _dot(n, w_ref[:, :d]).astype(BF16)
    v_ref[0] = _dot(n, w_ref[:, d:]).astype(BF16)


def _memkv(mem, gain, wkv):
    bsz, m_len, d = mem.shape
    blk = pl.BlockSpec((1, m_len, d), lambda b: (b, 0, 0))
    return pl.pallas_call(
        _memkv_kernel,
        grid=(bsz,),
        in_specs=[blk, _resident((1, d)), _resident((d, 2 * d))],
        out_specs=[blk, blk],
        out_shape=[jax.ShapeDtypeStruct((bsz, m_len, d), BF16)] * 2,
        compiler_params=_params("parallel"),
        name="memkv",
    )(mem, gain.reshape(1, d), wkv.astype(BF16))


def _mixout_xattn_kernel(h_ref, ys_ref, ya_ref, wout_ref, g_ref, wq_ref, km_ref, vm_ref, wo_ref,
                         o_ref, oh_ref, *, q_scale):
    h2 = h_ref[0] + _dot(jnp.concatenate([ys_ref[0], ya_ref[0]], axis=1), wout_ref[...])
    n = _rms(h2, g_ref[...]).astype(BF16)
    q = (_dot(n, wq_ref[...]) * q_scale).astype(BF16)
    d = q.shape[1]
    hd = d // XATTN_HEADS
    for hh in range(XATTN_HEADS):
        sl = slice(hh * hd, (hh + 1) * hd)
        s = _dot_nt(q[:, sl], km_ref[0, :, sl])
        p = jnp.exp(s - jnp.max(s, axis=-1, keepdims=True))
        o = _dot(p.astype(BF16), vm_ref[0, :, sl]) / jnp.sum(p, axis=-1, keepdims=True)
        oh_ref[:, sl] = o.astype(BF16)
    o_ref[0] = h2 + _dot(oh_ref[...], wo_ref[...])


def _mixout_xattn(h, y_ssm, y_attn, w_out, gain, wq, k_mem, v_mem, wo):
    bsz, seq, d = h.shape
    ws, wa = y_ssm.shape[2], y_attn.shape[2]
    m_len = k_mem.shape[1]
    tm = min(TOKEN_TILE, seq)
    tok = lambda w: pl.BlockSpec((1, tm, w), lambda b, i: (b, i, 0))
    mem = pl.BlockSpec((1, m_len, d), lambda b, i: (b, 0, 0))
    return pl.pallas_call(
        functools.partial(_mixout_xattn_kernel, q_scale=(d // XATTN_HEADS) ** -0.5),
        grid=(bsz, seq // tm),
        in_specs=[tok(d), tok(ws), tok(wa), _resident((ws + wa, d)), _resident((1, d)),
                  _resident((d, d)), mem, mem, _resident((d, d))],
        out_specs=tok(d),
        out_shape=jax.ShapeDtypeStruct((bsz, seq, d), F32),
        scratch_shapes=[pltpu.VMEM((tm, d), BF16)],
        compiler_params=_params("parallel", "parallel"),
        name="mixout_xattn",
    )(h, y_ssm, y_attn, w_out.astype(BF16), gain.reshape(1, d), wq.astype(BF16), k_mem, v_mem,
      wo.astype(BF16))


def kernel(x, mem, rel_bias, ffn1_norm, ffn1_w_gate, ffn1_w_up, ffn1_w_down, mix_norm, w_in,
           ssm_lambda_re, ssm_lambda_im, ssm_b_re, ssm_b_im, ssm_c_re, ssm_c_im, ssm_d,
           ssm_log_dt, ssm_w_glu, ssm_b_glu, ssm_out_norm, diff_lambda_q1, diff_lambda_k1,
           diff_lambda_q2, diff_lambda_k2, diff_subln, w_out, xattn_norm, mem_norm, xattn_wq,
           xattn_wkv, xattn_wo, ffn2_norm, ffn2_w_gate, ffn2_w_up, ffn2_w_down, final_norm):
    bsz, seq, d = x.shape
    depth = w_in.shape[0]
    width = w_in.shape[2] // 4
    dqk = width // DIFF_HEADS // 2
    t = bsz * seq
    assert t % TOKEN_TILE == 0 and seq % min(S5_TILE, seq) == 0 and seq % min(ATTN_TILE, seq) == 0
    h = x
    bias_tiles = _rel_bias_tiles(rel_bias, min(ATTN_TILE, seq))
    for l in range(depth):
        lam_init = 0.8 - 0.6 * math.exp(-0.3 * l)
        h = _ffn(h.reshape(t, d), ffn1_norm[l], ffn1_w_gate[l], ffn1_w_up[l], ffn1_w_down[l])
        u, q, k, v = _inproj(h, mix_norm[l], w_in[l], width, dqk ** -0.5 * LOG2E)
        shp = (bsz, seq, width)
        y_ssm = _s5(u.reshape(shp), ssm_lambda_re[l], ssm_lambda_im[l], ssm_b_re[l], ssm_b_im[l],
                    ssm_c_re[l], ssm_c_im[l], ssm_d[l], ssm_log_dt[l], ssm_w_glu[l], ssm_b_glu[l],
                    ssm_out_norm[l])
        lam_params = jnp.stack([diff_lambda_q1[l], diff_lambda_k1[l], diff_lambda_q2[l], diff_lambda_k2[l]])
        y_attn = _diff_attention(q.reshape(shp), k.reshape(shp), v.reshape(shp), bias_tiles,
                                 lam_params.astype(F32), diff_subln[l], lam_init)
        k_mem, v_mem = _memkv(mem, mem_norm[l], xattn_wkv[l])
        h = _mixout_xattn(h.reshape(bsz, seq, d), y_ssm, y_attn, w_out[l], xattn_norm[l], xattn_wq[l],
                          k_mem, v_mem, xattn_wo[l])
        h = _ffn(h.reshape(t, d), ffn2_norm[l], ffn2_w_gate[l], ffn2_w_up[l], ffn2_w_down[l],
                 final_gain=final_norm if l == depth - 1 else None)
    return h.reshape(bsz, seq, d)
```
